```python
import math
import jax, jax.numpy as jnp
from jax import lax
import numpy as np

D_MODEL = 1024
BATCH = 16
SEQ = 2048
DEPTH = 4
DEC_BATCH = 8
DEC_SEQ = 32
PAST_LEN = 1024

CHUNK = 64
N_EVEN = (DEPTH + 1) // 2
N_ODD = DEPTH // 2
PLE_DIM = 256
RMS_EPS = 1e-6
A_WIDTH = 512
A_CONV = 3
B_WIDTH = 512
S5_GROUP = 16
S5_GROUPS = B_WIDTH // S5_GROUP
S5_STATE = 64
S5_BLOCK = 128
C_HEADS = 8
C_HEAD_DIM = 64
C_WIDTH = C_HEADS * C_HEAD_DIM
Q_BLOCK = 128
D_HEADS = 8
D_WIDTH = 512
D_HEAD_DIM = D_WIDTH // D_HEADS
GMLP_CHUNK = 128
D_FF = 2816
FFN_CONV = 3

EVEN_IN = 3 * A_WIDTH + B_WIDTH
ODD_IN = 3 * C_WIDTH + C_HEADS + 2 * D_WIDTH
MASK_VALUE = -1e30

kernel_name = "hybrid_streaming_encoder_step"


def rmsnorm(x, g):
    xf = x.astype(jnp.float32)
    y = xf * lax.rsqrt(jnp.mean(xf * xf, axis=-1, keepdims=True) + RMS_EPS)
    return (y * g.astype(jnp.float32)).astype(x.dtype)


def causal_dwconv(x, hist, w):
    L = x.shape[1]
    xx = jnp.concatenate([hist.astype(x.dtype), x], axis=1)
    y = w[0] * xx[:, :L]
    for t in range(1, w.shape[0]):
        y = y + w[t] * xx[:, t:t + L]
    return y, xx[:, L:]


def s5_discretize(a_re, a_im, log_dt, b_re, b_im):
    a_re = a_re.astype(jnp.float32); a_im = a_im.astype(jnp.float32)
    b_re = b_re.astype(jnp.float32); b_im = b_im.astype(jnp.float32)
    dt = jnp.exp(log_dt.astype(jnp.float32))[:, None]
    mag = jnp.exp(a_re * dt)
    ab_re = mag * jnp.cos(a_im * dt)
    ab_im = mag * jnp.sin(a_im * dt)
    z_re = ab_re - 1.0
    den = a_re * a_re + a_im * a_im
    f_re = (z_re * a_re + ab_im * a_im) / den
    f_im = (ab_im * a_re - z_re * a_im) / den
    bb_re = f_re[..., None] * b_re - f_im[..., None] * b_im
    bb_im = f_re[..., None] * b_im + f_im[..., None] * b_re
    return ab_re, ab_im, bb_re, bb_im


def _linrec_combine(e1, e2):
    a1r, a1i, b1r, b1i = e1
    a2r, a2i, b2r, b2i = e2
    return (a1r * a2r - a1i * a2i, a1r * a2i + a1i * a2r,
            a2r * b1r - a2i * b1i + b2r, a2r * b1i + a2i * b1r + b2i)


def s5_mixer(u, h_re, h_im, a_re, a_im, log_dt, b_re, b_im, c_re, c_im, d_skip, w_glu):
    bt, L, _ = u.shape
    uf = u.astype(jnp.float32).reshape(bt, L, S5_GROUPS, S5_GROUP)
    ab_re, ab_im, bb_re, bb_im = s5_discretize(a_re, a_im, log_dt, b_re, b_im)
    c_re = c_re.astype(jnp.float32); c_im = c_im.astype(jnp.float32)
    d_skip = d_skip.astype(jnp.float32)

    def run(ug, hr0, hi0):
        bu_re = jnp.einsum('blgc,gpc->blgp', ug, bb_re)
        bu_im = jnp.einsum('blgc,gpc->blgp', ug, bb_im)
        bu_re = bu_re.at[:, 0].add(ab_re * hr0 - ab_im * hi0)
        bu_im = bu_im.at[:, 0].add(ab_re * hi0 + ab_im * hr0)
        a_r = jnp.broadcast_to(ab_re, bu_re.shape)
        a_i = jnp.broadcast_to(ab_im, bu_im.shape)
        _, _, hs_re, hs_im = lax.associative_scan(_linrec_combine, (a_r, a_i, bu_re, bu_im), axis=1)
        y = (jnp.einsum('blgp,gcp->blgc', hs_re, c_re)
             - jnp.einsum('blgp,gcp->blgc', hs_im, c_im) + d_skip * ug)
        return y, hs_re[:, -1], hs_im[:, -1]

    h_re = h_re.astype(jnp.float32); h_im = h_im.astype(jnp.float32)
    if L > S5_BLOCK and L % S5_BLOCK == 0:
        nb = L // S5_BLOCK
        ub = uf.reshape(bt, nb, S5_BLOCK, S5_GROUPS, S5_GROUP).swapaxes(0, 1)

        def step(carry, ug):
            y, hr, hi = run(ug, carry[0], carry[1])
            return (hr, hi), y

        (hr, hi), ys = lax.scan(step, (h_re, h_im), ub)
        y = ys.swapaxes(0, 1).reshape(bt, L, B_WIDTH)
    else:
        y, hr, hi = run(uf, h_re, h_im)
        y = y.reshape(bt, L, B_WIDTH)
    g = jax.nn.gelu(y)
    out = g * jax.nn.sigmoid(g @ w_glu.astype(jnp.float32))
    return out.astype(u.dtype), hr, hi


def fox_block(q, k, v, fq, fk, qpos, kpos):
    s = jnp.einsum('bqhd,bkhd->bhqk', q, k).astype(jnp.float32) * (C_HEAD_DIM ** -0.5)
    s = s + (jnp.transpose(fq, (0, 2, 1))[:, :, :, None] - jnp.transpose(fk, (0, 2, 1))[:, :, None, :])
    s = jnp.where((kpos[None, :] <= qpos[:, None])[None, None], s, MASK_VALUE)
    p = jax.nn.softmax(s, axis=-1)
    return jnp.einsum('bhqk,bkhd->bqhd', p.astype(v.dtype), v)


def fox_prompt(q, k, v, logf):
    bt, L = q.shape[:2]
    F = jnp.cumsum(logf.astype(jnp.float32), axis=1)
    pos = jnp.arange(L)
    nb = L // Q_BLOCK
    qb = q.reshape(bt, nb, Q_BLOCK, C_HEADS, C_HEAD_DIM).swapaxes(0, 1)
    fb = F.reshape(bt, nb, Q_BLOCK, C_HEADS).swapaxes(0, 1)
    pb = pos.reshape(nb, Q_BLOCK)
    out = lax.map(lambda a: fox_block(a[0], k, v, a[1], F, a[2], pos), (qb, fb, pb))
    return out.swapaxes(0, 1).reshape(bt, L, C_HEADS, C_HEAD_DIM)


def fox_sample(q, k, v, logf, k_past, v_past, logf_past):
    past = k_past.shape[1]
    L = q.shape[1]
    k_all = jnp.concatenate([k_past.astype(k.dtype), k], axis=1)
    v_all = jnp.concatenate([v_past.astype(v.dtype), v], axis=1)
    F = jnp.cumsum(jnp.concatenate([logf_past.astype(jnp.float32), logf.astype(jnp.float32)], axis=1), axis=1)
    qpos = past + jnp.arange(L)
    kpos = jnp.arange(past + L)
    return fox_block(q, k_all, v_all, F[:, past:], F, qpos, kpos)


def gmlp_mix(u, vd, g_v, w_s, b_s):
    bt, L, _ = u.shape
    lc = min(L, GMLP_CHUNK)
    nc = L // lc
    vn = rmsnorm(vd, g_v)
    ws = jnp.where(jnp.tril(jnp.ones((lc, lc), bool)), w_s[:, :lc, :lc], 0).astype(vn.dtype)
    vh = vn.reshape(bt, nc, lc, D_HEADS, D_HEAD_DIM)
    mixed = jnp.einsum('hts,bcshd->bcthd', ws, vh) + b_s[:, :lc].T[:, :, None].astype(vn.dtype)
    return u * mixed.reshape(bt, L, D_WIDTH), vn


def run_trunk(x, pe, st, W):
    bt, L, _ = x.shape
    prompt = st is None
    h = x
    conv_a, ssm_re, ssm_im, ks, vs, lfs, gvs, ffs = [], [], [], [], [], [], [], []
    for i in range(DEPTH):
        j = i // 2
        hn = rmsnorm(h, W['g_mix_pre'][i])
        if i % 2 == 0:
            if prompt:
                hist = jnp.zeros((bt, A_CONV - 1, A_WIDTH), x.dtype)
                h0r = jnp.zeros((bt, S5_GROUPS, S5_STATE), jnp.float32)
                h0i = jnp.zeros((bt, S5_GROUPS, S5_STATE), jnp.float32)
            else:
                hist, h0r, h0i = st['conv_a'][j], st['ssm_re'][j], st['ssm_im'][j]
            z = hn @ W['w_even_in'][j]
            gb, gc, xa, u = jnp.split(z, [A_WIDTH, 2 * A_WIDTH, 3 * A_WIDTH], axis=-1)
            yc, new_hist = causal_dwconv(gc * xa, hist, W['w_conv_a'][j])
            ya = gb * yc
            yb, nr, ni = s5_mixer(u, h0r, h0i, W['s5_a_re'][j], W['s5_a_im'][j], W['s5_log_dt'][j],
                                  W['s5_b_re'][j], W['s5_b_im'][j], W['s5_c_re'][j], W['s5_c_im'][j],
                                  W['s5_d'][j], W['w_glu'][j])
            y = jnp.concatenate([ya, yb], axis=-1) @ W['w_even_out'][j]
            conv_a.append(new_hist); ssm_re.append(nr); ssm_im.append(ni)
        else:
            z = hn @ W['w_odd_in'][j]
            q, k, v, fl, u, vd = jnp.split(
                z, [C_WIDTH, 2 * C_WIDTH, 3 * C_WIDTH, 3 * C_WIDTH + C_HEADS,
                    3 * C_WIDTH + C_HEADS + D_WIDTH], axis=-1)
            q = q.reshape(bt, L, C_HEADS, C_HEAD_DIM)
            k = k.reshape(bt, L, C_HEADS, C_HEAD_DIM)
            v = v.reshape(bt, L, C_HEADS, C_HEAD_DIM)
            logf = jax.nn.log_sigmoid((fl + W['b_forget'][j]).astype(jnp.float32))
            if prompt:
                att = fox_prompt(q, k, v, logf)
            else:
                att = fox_sample(q, k, v, logf, st['k'][j], st['v'][j], st['logf'][j])
            yd, vn = gmlp_mix(u, vd, W['g_gmlp_v'][j], W['w_spatial'][j], W['b_spatial'][j])
            y = jnp.concatenate([att.reshape(bt, L, C_WIDTH), yd], axis=-1) @ W['w_odd_out'][j]
            ks.append(k); vs.append(v); lfs.append(logf); gvs.append(vn)
        h = h + rmsnorm(y, W['g_mix_post'][i])
        hn = rmsnorm(h, W['g_ffn_pre'][i])
        fh = jnp.zeros((bt, FFN_CONV - 1, 2 * D_FF), x.dtype) if prompt else st['ffn'][i]
        up = hn @ W['w_ffn_up'][i]
        uc, new_f = causal_dwconv(up, fh, W['w_ffn_conv'][i])
        gt, val = jnp.split(uc, 2, axis=-1)
        y = (jax.nn.gelu(gt) * val) @ W['w_ffn_down'][i]
        h = h + rmsnorm(y, W['g_ffn_post'][i])
        h = h + jax.nn.sigmoid(h @ W['w_ple_gate'][i]) * (pe[i] @ W['w_ple'][i])
        ffs.append(new_f)
    new_state = {'conv_a': jnp.stack(conv_a), 'ssm_re': jnp.stack(ssm_re), 'ssm_im': jnp.stack(ssm_im),
                 'k': jnp.stack(ks), 'v': jnp.stack(vs), 'logf': jnp.stack(lfs),
                 'gmlp_v': jnp.stack(gvs), 'ffn': jnp.stack(ffs)}
    return h, new_state


def setup_inputs(seed: int = 0) -> dict:
    key = jax.random.key(seed)
    ks = iter(jax.random.split(key, 48))

    def nrm(shape, scale):
        return jax.random.normal(next(ks), shape, jnp.float32) * scale

    def gain(shape):
        return 1.0 + nrm(shape, 0.02)

    n_idx = jnp.arange(S5_STATE, dtype=jnp.float32)
    return {
        'x_prompt': nrm((BATCH, SEQ, D_MODEL), 1.0),
        'x_sample': nrm((DEC_BATCH, DEC_SEQ, D_MODEL), 1.0),
        'p_prompt': nrm((DEPTH, BATCH, SEQ, PLE_DIM), 1.0),
        'p_sample': nrm((DEPTH, DEC_BATCH, DEC_SEQ, PLE_DIM), 1.0),
        'cache_conv_a': nrm((N_EVEN, DEC_BATCH, A_CONV - 1, A_WIDTH), 1.0),
        'state_ssm_re': nrm((N_EVEN, DEC_BATCH, S5_GROUPS, S5_STATE), 0.1),
        'state_ssm_im': nrm((N_EVEN, DEC_BATCH, S5_GROUPS, S5_STATE), 0.1),
        'cache_k': nrm((N_ODD, DEC_BATCH, PAST_LEN, C_HEADS, C_HEAD_DIM), 1.0),
        'cache_v': nrm((N_ODD, DEC_BATCH, PAST_LEN, C_HEADS, C_HEAD_DIM), 1.0),
        'cache_logf': jax.nn.log_sigmoid(2.0 + nrm((N_ODD, DEC_BATCH, PAST_LEN, C_HEADS), 0.5)),
        'cache_ffn_conv': nrm((DEPTH, DEC_BATCH, FFN_CONV - 1, 2 * D_FF), 1.0),
        'g_mix_pre': gain((DEPTH, D_MODEL)),
        'g_mix_post': gain((DEPTH, D_MODEL)),
        'g_ffn_pre': gain((DEPTH, D_MODEL)),
        'g_ffn_post': gain((DEPTH, D_MODEL)),
        'w_even_in': nrm((N_EVEN, D_MODEL, EVEN_IN), D_MODEL ** -0.5),
        'w_conv_a': nrm((N_EVEN, A_CONV, A_WIDTH), A_CONV ** -0.5),
        's5_a_re': -0.5 + nrm((N_EVEN, S5_GROUPS, S5_STATE), 0.01),
        's5_a_im': jnp.pi * n_idx + nrm((N_EVEN, S5_GROUPS, S5_STATE), 0.01),
        's5_log_dt': jax.random.uniform(next(ks), (N_EVEN, S5_GROUPS), jnp.float32,
                                        math.log(1e-3), math.log(1e-1)),
        's5_b_re': nrm((N_EVEN, S5_GROUPS, S5_STATE, S5_GROUP), (2 * S5_GROUP) ** -0.5),
        's5_b_im': nrm((N_EVEN, S5_GROUPS, S5_STATE, S5_GROUP), (2 * S5_GROUP) ** -0.5),
        's5_c_re': nrm((N_EVEN, S5_GROUPS, S5_GROUP, S5_STATE), (2 * S5_STATE) ** -0.5),
        's5_c_im': nrm((N_EVEN, S5_GROUPS, S5_GROUP, S5_STATE), (2 * S5_STATE) ** -0.5),
        's5_d': nrm((N_EVEN, S5_GROUPS, S5_GROUP), 1.0),
        'w_glu': nrm((N_EVEN, B_WIDTH, B_WIDTH), B_WIDTH ** -0.5),
        'w_even_out': nrm((N_EVEN, A_WIDTH + B_WIDTH, D_MODEL), (A_WIDTH + B_WIDTH) ** -0.5),
        'w_odd_in': nrm((N_ODD, D_MODEL, ODD_IN), D_MODEL ** -0.5),
        'b_forget': 2.0 + nrm((N_ODD, C_HEADS), 0.5),
        'w_spatial': nrm((N_ODD, D_HEADS, GMLP_CHUNK, GMLP_CHUNK), GMLP_CHUNK ** -0.5),
        'b_spatial': 1.0 + nrm((N_ODD, D_HEADS, GMLP_CHUNK), 0.01),
        'g_gmlp_v': gain((N_ODD, D_WIDTH)),
        'w_odd_out': nrm((N_ODD, C_WIDTH + D_WIDTH, D_MODEL), (C_WIDTH + D_WIDTH) ** -0.5),
        'w_ffn_up': nrm((DEPTH, D_MODEL, 2 * D_FF), D_MODEL ** -0.5),
        'w_ffn_conv': nrm((DEPTH, FFN_CONV, 2 * D_FF), FFN_CONV ** -0.5),
        'w_ffn_down': nrm((DEPTH, D_FF, D_MODEL), D_FF ** -0.5),
        'w_ple': nrm((DEPTH, PLE_DIM, D_MODEL), PLE_DIM ** -0.5),
        'w_ple_gate': nrm((DEPTH, D_MODEL, D_MODEL), D_MODEL ** -0.5),
    }


def reference(x_prompt, x_sample, p_prompt, p_sample, cache_conv_a, state_ssm_re, state_ssm_im,
              cache_k, cache_v, cache_logf, cache_ffn_conv,
              g_mix_pre, g_mix_post, g_ffn_pre, g_ffn_post,
              w_even_in, w_conv_a, s5_a_re, s5_a_im, s5_log_dt, s5_b_re, s5_b_im, s5_c_re, s5_c_im,
              s5_d, w_glu, w_even_out,
              w_odd_in, b_forget, w_spatial, b_spatial, g_gmlp_v, w_odd_out,
              w_ffn_up, w_ffn_conv, w_ffn_down, w_ple, w_ple_gate):
    W = {'g_mix_pre': g_mix_pre, 'g_mix_post': g_mix_post, 'g_ffn_pre': g_ffn_pre, 'g_ffn_post': g_ffn_post,
         'w_even_in': w_even_in, 'w_conv_a': w_conv_a, 's5_a_re': s5_a_re, 's5_a_im': s5_a_im,
         's5_log_dt': s5_log_dt, 's5_b_re': s5_b_re, 's5_b_im': s5_b_im, 's5_c_re': s5_c_re,
         's5_c_im': s5_c_im, 's5_d': s5_d, 'w_glu': w_glu, 'w_even_out': w_even_out,
         'w_odd_in': w_odd_in, 'b_forget': b_forget, 'w_spatial': w_spatial, 'b_spatial': b_spatial,
         'g_gmlp_v': g_gmlp_v, 'w_odd_out': w_odd_out, 'w_ffn_up': w_ffn_up, 'w_ffn_conv': w_ffn_conv,
         'w_ffn_down': w_ffn_down, 'w_ple': w_ple, 'w_ple_gate': w_ple_gate}
    y_prompt, sp = run_trunk(x_prompt, p_prompt, None, W)
    st = {'conv_a': cache_conv_a, 'ssm_re': state_ssm_re, 'ssm_im': state_ssm_im,
          'k': cache_k, 'v': cache_v, 'logf': cache_logf, 'ffn': cache_ffn_conv}
    y_sample, ss = run_trunk(x_sample, p_sample, st, W)
    return (y_prompt, y_sample,
            sp['conv_a'], sp['ssm_re'], sp['ssm_im'], sp['k'], sp['v'], sp['logf'], sp['ffn'],
            ss['conv_a'], ss['ssm_re'], ss['ssm_im'], ss['k'], ss['v'], ss['logf'], ss['gmlp_v'], ss['ffn'])
```

```python
import functools

import jax
import jax.numpy as jnp
from jax import lax
from jax.experimental import pallas as pl
from jax.experimental.pallas import tpu as pltpu

F32 = jnp.float32
BF16 = jnp.bfloat16

D_MODEL = 1024
DEPTH = 4
PLE_DIM = 256
RMS_EPS = 1e-6
A_WIDTH = 512
A_CONV = 3
B_WIDTH = 512
S5_GROUP = 16
S5_GROUPS = 32
S5_STATE = 64
S5_PAIRS = S5_GROUPS // 2
S5_QUADS = B_WIDTH // 128
C_HEADS = 8
C_HEAD_DIM = 64
C_WIDTH = 512
HEAD_PAIRS = C_HEADS // 2
D_WIDTH = 512
GMLP_CHUNK = 128
D_FF = 2816
FFN_CONV = 3
FF_CHUNK = 256
N_FF_CHUNKS = D_FF // FF_CHUNK
MASK_VALUE = -1e30
CONV_PAD = 8
HIST = A_CONV - 1

V7X_VMEM_LIMIT = 56 * 1024 * 1024
LANES = 128


def _rms(x, g):
    return x * lax.rsqrt(jnp.mean(x * x, axis=-1, keepdims=True) + RMS_EPS) * g


def _dot(a, b):
    return jnp.dot(a, b, preferred_element_type=F32)


def _dot_nt(a, b):
    return lax.dot_general(a, b, (((1,), (1,)), ((), ())), preferred_element_type=F32)


def _dot_exact(a, b):
    return jnp.dot(a, b, preferred_element_type=F32, precision=lax.Precision.HIGHEST)


def _const_spec(shape):
    zeros = (0,) * len(shape)
    return pl.BlockSpec(shape, lambda *_: zeros, pipeline_mode=pl.Buffered(1))


def _params(*sem):
    return pltpu.CompilerParams(dimension_semantics=sem, vmem_limit_bytes=V7X_VMEM_LIMIT)


def _causal_conv3(cb_ref, cur, hist, w, tT):
    cb_ref[:, CONV_PAD - HIST:CONV_PAD, :] = hist
    cb_ref[:, CONV_PAD:CONV_PAD + tT, :] = cur
    y = w[0:1][None] * cb_ref[:, CONV_PAD - 2:CONV_PAD - 2 + tT, :]
    y = y + w[1:2][None] * cb_ref[:, CONV_PAD - 1:CONV_PAD - 1 + tT, :]
    y = y + w[2:3][None] * cur
    return y, cb_ref[:, CONV_PAD + tT - HIST:CONV_PAD + tT, :]


def _even_in_kernel(h_ref, g_ref, w_ref, wc_ref, hist_ref, ya_ref, u_ref, nh_ref, cb_ref, *, bB, tT):
    @pl.when(pl.program_id(1) == 0)
    def _():
        nh_ref[...] = hist_ref[...]

    x = h_ref[...].reshape(bB * tT, D_MODEL)
    hn = _rms(x, g_ref[...])
    z = _dot(hn.astype(BF16), w_ref[...])
    gb = z[:, 0:A_WIDTH]
    gc = z[:, A_WIDTH:2 * A_WIDTH]
    xa = z[:, 2 * A_WIDTH:3 * A_WIDTH]
    u = z[:, 3 * A_WIDTH:]
    cx = (gc * xa).reshape(bB, tT, A_WIDTH)
    yc, last = _causal_conv3(cb_ref, cx, nh_ref[...], wc_ref[...], tT)
    nh_ref[...] = last
    ya_ref[...] = (gb.reshape(bB, tT, A_WIDTH) * yc).astype(BF16)
    for b in range(bB):
        u_ref[:, b * B_WIDTH:(b + 1) * B_WIDTH] = u[b * tT:(b + 1) * tT]


def _even_in(h, g, w, wc, hist, bB, tT):
    B, L, _ = h.shape
    grid = (B // bB, L // tT)
    kern = functools.partial(_even_in_kernel, bB=bB, tT=tT)
    return pl.pallas_call(
        kern,
        grid=grid,
        in_specs=[
            pl.BlockSpec((bB, tT, D_MODEL), lambda b, t: (b, t, 0)),
            _const_spec((1, D_MODEL)),
            _const_spec((D_MODEL, 4 * A_WIDTH)),
            _const_spec((A_CONV, A_WIDTH)),
            pl.BlockSpec((bB, HIST, A_WIDTH), lambda b, t: (b, 0, 0)),
        ],
        out_specs=[
            pl.BlockSpec((bB, tT, A_WIDTH), lambda b, t: (b, t, 0)),
            pl.BlockSpec((tT, bB * B_WIDTH), lambda b, t: (t, b)),
            pl.BlockSpec((bB, HIST, A_WIDTH), lambda b, t: (b, 0, 0)),
        ],
        out_shape=[
            jax.ShapeDtypeStruct((B, L, A_WIDTH), BF16),
            jax.ShapeDtypeStruct((L, B * B_WIDTH), F32),
            jax.ShapeDtypeStruct((B, HIST, A_WIDTH), F32),
        ],
        scratch_shapes=[pltpu.VMEM((bB, tT + CONV_PAD, A_WIDTH), F32)],
        compiler_params=_params("arbitrary", "arbitrary"),
        name="even_in",
    )(h, g, w, wc, hist)


S5_SCAN_PAIRS = 4


def _s5_kernel(u_ref, are_ref, aim_ref, ldt_ref, bre_ref, bim_ref, cre_ref, cim_ref, d_ref, wglu_ref,
               h0r_ref, h0i_ref, yb_ref, hr_ref, hi_ref, lam_ref, wb_ref, wc_ref, sre_ref, sim_ref, *, B, tT):
    M = tT * B

    @pl.when(pl.program_id(0) == 0)
    def _():
        a_re = are_ref[...]
        a_im = aim_ref[...]
        dt = jnp.exp(ldt_ref[...])
        mag = jnp.exp(a_re * dt)
        ab_re = mag * jnp.cos(a_im * dt)
        ab_im = mag * jnp.sin(a_im * dt)
        z_re = ab_re - 1.0
        den = a_re * a_re + a_im * a_im
        f_re = (z_re * a_re + ab_im * a_im) / den
        f_im = (ab_im * a_re - z_re * a_im) / den
        lam_ref[0] = ab_re
        lam_ref[1] = ab_im
        for j in range(S5_PAIRS):
            fr = f_re[j:j + 1]
            fi = f_im[j:j + 1]
            br = bre_ref[j]
            bi = bim_ref[j]
            wb_ref[j, :, 0:LANES] = (fr * br - fi * bi).astype(BF16)
            wb_ref[j, :, LANES:2 * LANES] = (fr * bi + fi * br).astype(BF16)
            wc_ref[j, 0:LANES, :] = cre_ref[j].astype(BF16)
            wc_ref[j, LANES:2 * LANES, :] = (-cim_ref[j]).astype(BF16)
        hr_ref[...] = h0r_ref[...]
        hi_ref[...] = h0i_ref[...]

    u = u_ref[...].reshape(M, B_WIDTH)
    ub = u.astype(BF16)
    for j in range(S5_PAIRS):
        q = j // (S5_PAIRS // S5_QUADS)
        bu = _dot(ub[:, q * LANES:(q + 1) * LANES], wb_ref[j])
        sre_ref[j] = bu[:, 0:LANES]
        sim_ref[j] = bu[:, LANES:2 * LANES]

    for j0 in range(0, S5_PAIRS, S5_SCAN_PAIRS):
        pairs = range(j0, j0 + S5_SCAN_PAIRS)
        lr = [lam_ref[0, j:j + 1, :] for j in pairs]
        li = [lam_ref[1, j:j + 1, :] for j in pairs]
        hr0 = tuple(hr_ref[:, j * LANES:(j + 1) * LANES] for j in pairs)
        hi0 = tuple(hi_ref[:, j * LANES:(j + 1) * LANES] for j in pairs)

        def step(t, carry, lr=lr, li=li, pairs=pairs):
            hr, hi = carry
            row = pl.multiple_of(t * B, B)
            nr, ni = [], []
            for k, j in enumerate(pairs):
                br = sre_ref[j, pl.ds(row, B), :]
                bi = sim_ref[j, pl.ds(row, B), :]
                r = lr[k] * hr[k] - li[k] * hi[k] + br
                i = lr[k] * hi[k] + li[k] * hr[k] + bi
                sre_ref[j, pl.ds(row, B), :] = r
                sim_ref[j, pl.ds(row, B), :] = i
                nr.append(r)
                ni.append(i)
            return tuple(nr), tuple(ni)

        hr, hi = lax.fori_loop(0, tT, step, (hr0, hi0))
        for k, j in enumerate(pairs):
            hr_ref[:, j * LANES:(j + 1) * LANES] = hr[k]
            hi_ref[:, j * LANES:(j + 1) * LANES] = hi[k]

    d = d_ref[...]
    cols = []
    per_quad = S5_PAIRS // S5_QUADS
    for q in range(S5_QUADS):
        acc = None
        for j in range(q * per_quad, (q + 1) * per_quad):
            hs = jnp.concatenate([sre_ref[j], sim_ref[j]], axis=-1).astype(BF16)
            part = _dot(hs, wc_ref[j])
            acc = part if acc is None else acc + part
        cols.append(acc + d[:, q * LANES:(q + 1) * LANES] * u[:, q * LANES:(q + 1) * LANES])
    y = jnp.concatenate(cols, axis=-1)
    g = jax.nn.gelu(y)
    out = g * jax.nn.sigmoid(_dot(g.astype(BF16), wglu_ref[...]))
    yb_ref[...] = out.reshape(tT, B, B_WIDTH)


def _s5(u_tm, prm, h0r, h0i, B, L, tT):
    M = tT * B
    state = S5_PAIRS * LANES
    kern = functools.partial(_s5_kernel, B=B, tT=tT)
    tile = lambda i: (i, 0, 0)
    return pl.pallas_call(
        kern,
        grid=(L // tT,),
        in_specs=[
            pl.BlockSpec((tT, B, B_WIDTH), tile),
            _const_spec((S5_PAIRS, LANES)),
            _const_spec((S5_PAIRS, LANES)),
            _const_spec((S5_PAIRS, LANES)),
            _const_spec((S5_PAIRS, LANES, LANES)),
            _const_spec((S5_PAIRS, LANES, LANES)),
            _const_spec((S5_PAIRS, LANES, LANES)),
            _const_spec((S5_PAIRS, LANES, LANES)),
            _const_spec((1, B_WIDTH)),
            _const_spec((B_WIDTH, B_WIDTH)),
            _const_spec((B, state)),
            _const_spec((B, state)),
        ],
        out_specs=[
            pl.BlockSpec((tT, B, B_WIDTH), tile),
            pl.BlockSpec((B, state), lambda i: (0, 0)),
            pl.BlockSpec((B, state), lambda i: (0, 0)),
        ],
        out_shape=[
            jax.ShapeDtypeStruct((L, B, B_WIDTH), F32),
            jax.ShapeDtypeStruct((B, state), F32),
            jax.ShapeDtypeStruct((B, state), F32),
        ],
        scratch_shapes=[
            pltpu.VMEM((2, S5_PAIRS, LANES), F32),
            pltpu.VMEM((S5_PAIRS, LANES, 2 * LANES), BF16),
            pltpu.VMEM((S5_PAIRS, 2 * LANES, LANES), BF16),
            pltpu.VMEM((S5_PAIRS, M, LANES), F32),
            pltpu.VMEM((S5_PAIRS, M, LANES), F32),
        ],
        compiler_params=_params("arbitrary"),
        name="s5_scan",
    )(u_tm.reshape(L, B, B_WIDTH), prm["a_re"], prm["a_im"], prm["ldt"], prm["b_re"], prm["b_im"],
      prm["c_re"], prm["c_im"], prm["d"], prm["w_glu"], h0r, h0i)


def _odd_in_kernel(h_ref, g_ref, w_ref, wf_ref, wft_ref, bf_ref, bft_ref, gv_ref, ws_ref, bs_ref,
                   q_ref, k_ref, v_ref, kb_ref, vb_ref, lf_ref, fc_ref, ft_ref, yd_ref, vn_ref,
                   cc_ref, cr_ref, *, bB, tT, lc):
    M = bB * tT
    t = pl.program_id(1)

    @pl.when(t == 0)
    def _():
        cc_ref[...] = jnp.zeros_like(cc_ref)
        cr_ref[...] = jnp.zeros_like(cr_ref)

    x = h_ref[...].reshape(M, D_MODEL)
    hn = _rms(x, g_ref[...]).astype(BF16)
    z = _dot(hn, w_ref[...])
    q = z[:, 0:C_WIDTH] * (C_HEAD_DIM ** -0.5)
    k = z[:, C_WIDTH:2 * C_WIDTH]
    v = z[:, 2 * C_WIDTH:3 * C_WIDTH]
    u = z[:, 3 * C_WIDTH:3 * C_WIDTH + D_WIDTH]
    vd = z[:, 3 * C_WIDTH + D_WIDTH:]
    q_ref[...] = q.reshape(bB, tT, C_WIDTH).astype(BF16)
    k_ref[...] = k.reshape(bB, tT, C_WIDTH)
    v_ref[...] = v.reshape(bB, tT, C_WIDTH)
    kb_ref[...] = k.reshape(bB, tT, C_WIDTH).astype(BF16)
    vb_ref[...] = v.reshape(bB, tT, C_WIDTH).astype(BF16)

    logf = jax.nn.log_sigmoid(_dot(hn, wf_ref[...]) + bf_ref[...])
    logf_t = jax.nn.log_sigmoid(_dot_nt(wft_ref[...], hn) + bft_ref[...])
    lf_ref[...] = logf.reshape(bB, tT, C_HEADS)
    r = lax.broadcasted_iota(jnp.int32, (M, M), 0)
    c = lax.broadcasted_iota(jnp.int32, (M, M), 1)
    same = (r // tT) == (c // tT)
    lower = jnp.where(same & (c <= r), 1.0, 0.0).astype(F32)
    upper = jnp.where(same & (r <= c), 1.0, 0.0).astype(F32)
    fc = _dot_exact(lower, logf) + cc_ref[...]
    ft = _dot_exact(logf_t, upper) + cr_ref[...]
    fc_ref[...] = fc.reshape(bB, tT, C_HEADS)
    ft_ref[0] = ft
    if bB == 1:
        cc_ref[...] = fc[M - 1:M, :]
        cr_ref[...] = ft[:, M - 1:M]

    vn = _rms(vd, gv_ref[...])
    vn_ref[...] = vn.reshape(bB, tT, D_WIDTH)
    vnb = vn.astype(BF16)
    rr = lax.broadcasted_iota(jnp.int32, (lc, lc), 0)
    cc = lax.broadcasted_iota(jnp.int32, (lc, lc), 1)
    wsm = [jnp.where(cc <= rr, ws_ref[hh], 0.0).astype(BF16) for hh in range(C_HEADS)]
    first_head = lax.broadcasted_iota(jnp.int32, (lc, LANES), 1) < C_HEAD_DIM
    bias = bs_ref[...]
    for ch in range(M // lc):
        rows = slice(ch * lc, (ch + 1) * lc)
        cols = []
        for hp in range(HEAD_PAIRS):
            vp = vnb[rows, hp * LANES:(hp + 1) * LANES]
            cols.append(jnp.where(first_head, _dot(wsm[2 * hp], vp), _dot(wsm[2 * hp + 1], vp)))
        mixed = jnp.concatenate(cols, axis=-1) + bias
        b, off = divmod(ch * lc, tT)
        yd_ref[b, off:off + lc, :] = (u[rows] * mixed).astype(BF16)


def _odd_in(h, g, w, wf, wft, bf, bft, gv, ws, bs, bB, tT):
    B, L, _ = h.shape
    nT = L // tT
    assert bB == 1 or nT == 1
    lc = ws.shape[-1]
    grid = (B // bB, nT)
    kern = functools.partial(_odd_in_kernel, bB=bB, tT=tT, lc=lc)
    tile = lambda b, t: (b, t, 0)
    n_in = 3 * C_WIDTH + 2 * D_WIDTH

    def act(width, dtype):
        return pl.BlockSpec((bB, tT, width), tile), jax.ShapeDtypeStruct((B, L, width), dtype)

    outs = [act(C_WIDTH, BF16), act(C_WIDTH, F32), act(C_WIDTH, F32), act(C_WIDTH, BF16), act(C_WIDTH, BF16),
            act(C_HEADS, F32), act(C_HEADS, F32),
            (pl.BlockSpec((1, C_HEADS, bB * tT), lambda b, t: (b * nT + t, 0, 0)),
             jax.ShapeDtypeStruct((B * L // (bB * tT), C_HEADS, bB * tT), F32)),
            act(D_WIDTH, BF16), act(D_WIDTH, F32)]
    return pl.pallas_call(
        kern,
        grid=grid,
        in_specs=[
            pl.BlockSpec((bB, tT, D_MODEL), tile),
            _const_spec((1, D_MODEL)),
            _const_spec((D_MODEL, n_in)),
            _const_spec((D_MODEL, C_HEADS)),
            _const_spec((C_HEADS, D_MODEL)),
            _const_spec((1, C_HEADS)),
            _const_spec((C_HEADS, 1)),
            _const_spec((1, D_WIDTH)),
            _const_spec((C_HEADS, lc, lc)),
            _const_spec((lc, D_WIDTH)),
        ],
        out_specs=[o[0] for o in outs],
        out_shape=[o[1] for o in outs],
        scratch_shapes=[pltpu.VMEM((1, C_HEADS), F32), pltpu.VMEM((C_HEADS, 1), F32)],
        compiler_params=_params("arbitrary", "arbitrary"),
        name="odd_in",
    )(h, g, w, wf, wft, bf, bft, gv, ws, bs)


def _softmax_pair_update(carry, s0, s1, vblk, first_head):
    m0, m1, l0, l1, acc = carry
    n0 = jnp.maximum(m0, jnp.max(s0, axis=-1, keepdims=True))
    n1 = jnp.maximum(m1, jnp.max(s1, axis=-1, keepdims=True))
    a0 = jnp.exp(m0 - n0)
    a1 = jnp.exp(m1 - n1)
    p0 = jnp.exp(s0 - n0)
    p1 = jnp.exp(s1 - n1)
    l0 = a0 * l0 + jnp.sum(p0, axis=-1, keepdims=True)
    l1 = a1 * l1 + jnp.sum(p1, axis=-1, keepdims=True)
    pv = jnp.where(first_head, _dot(p0.astype(BF16), vblk), _dot(p1.astype(BF16), vblk))
    acc = acc * jnp.where(first_head, a0, a1) + pv
    return n0, n1, l0, l1, acc


def _attn_prompt_kernel(q_ref, k_ref, v_ref, fc_ref, ft_ref, o_ref, *, tq, tk):
    qi = pl.program_id(1)
    first_head = lax.broadcasted_iota(jnp.int32, (tq, LANES), 1) < C_HEAD_DIM
    qpos = qi * tq + lax.broadcasted_iota(jnp.int32, (tq, tk), 0)
    kofs = lax.broadcasted_iota(jnp.int32, (tq, tk), 1)
    n_blocks = (qi + 1) * (tq // tk)
    for hp in range(HEAD_PAIRS):
        lanes = slice(hp * LANES, (hp + 1) * LANES)
        qp = q_ref[0, :, lanes]
        q0 = jnp.where(first_head, qp, jnp.zeros_like(qp))
        q1 = jnp.where(first_head, jnp.zeros_like(qp), qp)
        fq0 = fc_ref[0, :, 2 * hp:2 * hp + 1]
        fq1 = fc_ref[0, :, 2 * hp + 1:2 * hp + 2]

        def body(j, carry, lanes=lanes, q0=q0, q1=q1, fq0=fq0, fq1=fq1, hp=hp):
            start = pl.multiple_of(j * tk, tk)
            kblk = k_ref[0, pl.ds(start, tk), lanes]
            vblk = v_ref[0, pl.ds(start, tk), lanes]
            fk = ft_ref[j]
            visible = (start + kofs) <= qpos
            s0 = jnp.where(visible, _dot_nt(q0, kblk) + (fq0 - fk[2 * hp:2 * hp + 1]), MASK_VALUE)
            s1 = jnp.where(visible, _dot_nt(q1, kblk) + (fq1 - fk[2 * hp + 1:2 * hp + 2]), MASK_VALUE)
            return _softmax_pair_update(carry, s0, s1, vblk, first_head)

        neg = jnp.full((tq, 1), -jnp.inf, F32)
        zero = jnp.zeros((tq, 1), F32)
        init = (neg, neg, zero, zero, jnp.zeros((tq, LANES), F32))
        _, _, l0, l1, acc = lax.fori_loop(0, n_blocks, body, init)
        o_ref[0, :, lanes] = (acc / jnp.where(first_head, l0, l1)).astype(BF16)


def _attn_prompt(q, kb, vb, fc, ft, tq, tk):
    B, L, _ = q.shape
    kern = functools.partial(_attn_prompt_kernel, tq=tq, tk=tk)
    return pl.pallas_call(
        kern,
        grid=(B, L // tq),
        in_specs=[
            pl.BlockSpec((1, tq, C_WIDTH), lambda b, i: (b, i, 0)),
            pl.BlockSpec((1, L, C_WIDTH), lambda b, i: (b, 0, 0)),
            pl.BlockSpec((1, L, C_WIDTH), lambda b, i: (b, 0, 0)),
            pl.BlockSpec((1, tq, C_HEADS), lambda b, i: (b, i, 0)),
            pl.BlockSpec((L // tk, C_HEADS, tk), lambda b, i: (b, 0, 0)),
        ],
        out_specs=pl.BlockSpec((1, tq, C_WIDTH), lambda b, i: (b, i, 0)),
        out_shape=jax.ShapeDtypeStruct((B, L, C_WIDTH), BF16),
        compiler_params=_params("arbitrary", "arbitrary"),
        name="attn_prompt",
    )(q, kb, vb, fc, ft)


def _attn_sample_kernel(q_ref, kn_ref, vn_ref, kp_ref, vp_ref, lpt_ref, fc_ref, ft_ref, o_ref, *, L, P):
    first_head = lax.broadcasted_iota(jnp.int32, (L, LANES), 1) < C_HEAD_DIM
    r = lax.broadcasted_iota(jnp.int32, (P, P), 0)
    c = lax.broadcasted_iota(jnp.int32, (P, P), 1)
    upper = jnp.where(r <= c, 1.0, 0.0).astype(F32)
    past_cum = _dot_exact(lpt_ref[0], upper)
    past_after = past_cum[:, P - 1:P] - past_cum
    causal = lax.broadcasted_iota(jnp.int32, (L, L), 1) <= lax.broadcasted_iota(jnp.int32, (L, L), 0)
    fc = fc_ref[0]
    ft = ft_ref[0]
    for hp in range(HEAD_PAIRS):
        lanes = slice(hp * LANES, (hp + 1) * LANES)
        qp = q_ref[0, :, lanes]
        qs = (jnp.where(first_head, qp, jnp.zeros_like(qp)), jnp.where(first_head, jnp.zeros_like(qp), qp))
        kp = kp_ref[0, :, lanes].astype(BF16)
        vp = vp_ref[0, :, lanes].astype(BF16)
        kn = kn_ref[0, :, lanes]
        vn = vn_ref[0, :, lanes]
        outs = []
        for e in range(2):
            h = 2 * hp + e
            fq = fc[:, h:h + 1]
            sp = _dot_nt(qs[e], kp) + (fq + past_after[h:h + 1])
            sn = jnp.where(causal, _dot_nt(qs[e], kn) + (fq - ft[h:h + 1]), MASK_VALUE)
            m = jnp.maximum(jnp.max(sp, axis=-1, keepdims=True), jnp.max(sn, axis=-1, keepdims=True))
            pp = jnp.exp(sp - m)
            pn = jnp.exp(sn - m)
            den = jnp.sum(pp, axis=-1, keepdims=True) + jnp.sum(pn, axis=-1, keepdims=True)
            outs.append((_dot(pp.astype(BF16), vp) + _dot(pn.astype(BF16), vn)) / den)
        o_ref[0, :, lanes] = jnp.where(first_head, outs[0], outs[1]).astype(BF16)


def _attn_sample(q, kb, vb, k_past, v_past, logf_past_t, fc, ft):
    B, L, _ = q.shape
    P = k_past.shape[1]
    kern = functools.partial(_attn_sample_kernel, L=L, P=P)
    new = pl.BlockSpec((1, L, C_WIDTH), lambda b: (b, 0, 0))
    past = pl.BlockSpec((1, P, C_WIDTH), lambda b: (b, 0, 0))
    return pl.pallas_call(
        kern,
        grid=(B,),
        in_specs=[new, new, new, past, past,
                  pl.BlockSpec((1, C_HEADS, P), lambda b: (b, 0, 0)),
                  pl.BlockSpec((1, L, C_HEADS), lambda b: (b, 0, 0)),
                  pl.BlockSpec((1, C_HEADS, L), lambda b: (b, 0, 0))],
        out_specs=new,
        out_shape=jax.ShapeDtypeStruct((B, L, C_WIDTH), BF16),
        compiler_params=_params("arbitrary"),
        name="attn_sample",
    )(q, kb, vb, k_past, v_past, logf_past_t, fc, ft)


def _ffn_kernel(h_ref, a_ref, b_ref, wo_ref, gpost_ref, gpre_ref, wup_ref, wcv_ref, fh_ref, wdn_ref, gfp_ref,
                wgate_ref, pe_ref, wple_ref, o_ref, nf_ref, cb_ref, acc_ref, *, bB, tT, b_time_major):
    M = bB * tT

    @pl.when(pl.program_id(1) == 0)
    def _():
        nf_ref[...] = fh_ref[...]

    a = a_ref[...].reshape(M, a_ref.shape[-1])
    if b_time_major:
        width = b_ref.shape[-1] // bB
        bpart = jnp.concatenate([b_ref[:, i * width:(i + 1) * width] for i in range(bB)], axis=0)
    else:
        bpart = b_ref[...].reshape(M, b_ref.shape[-1])
    ka = a.shape[-1]
    y = _dot(a.astype(BF16), wo_ref[0:ka, :]) + _dot(bpart.astype(BF16), wo_ref[ka:, :])
    h1 = h_ref[...].reshape(M, D_MODEL) + _rms(y, gpost_ref[...])
    hn = _rms(h1, gpre_ref[...]).astype(BF16)

    acc_ref[...] = jnp.zeros_like(acc_ref)

    def chunk(c, _):
        up = _dot(hn, wup_ref[c]).reshape(bB, tT, 2 * FF_CHUNK)
        uc, last = _causal_conv3(cb_ref, up, nf_ref[c], wcv_ref[c], tT)
        nf_ref[c] = last
        act = jax.nn.gelu(uc[:, :, 0:FF_CHUNK]) * uc[:, :, FF_CHUNK:]
        acc_ref[...] += _dot(act.reshape(M, FF_CHUNK).astype(BF16), wdn_ref[c])
        return 0

    lax.fori_loop(0, N_FF_CHUNKS, chunk, 0)

    h2 = h1 + _rms(acc_ref[...], gfp_ref[...])
    gate = jax.nn.sigmoid(_dot(h2.astype(BF16), wgate_ref[...]))
    pe = pe_ref[...].reshape(M, PLE_DIM).astype(BF16)
    h3 = h2 + gate * _dot(pe, wple_ref[...])
    o_ref[...] = h3.reshape(bB, tT, D_MODEL)


def _ffn(h, a, b, b_time_major, lw, fh, pe, bB, tT):
    B, L, _ = h.shape
    grid = (B // bB, L // tT)
    kern = functools.partial(_ffn_kernel, bB=bB, tT=tT, b_time_major=b_time_major)
    tile = lambda bi, t: (bi, t, 0)
    wa = a.shape[-1]
    if b_time_major:
        wb = b.shape[-1] // B
        b_spec = pl.BlockSpec((tT, bB * wb), lambda bi, t: (t, bi))
    else:
        wb = b.shape[-1]
        b_spec = pl.BlockSpec((bB, tT, wb), tile)
    state = pl.BlockSpec((N_FF_CHUNKS, bB, HIST, 2 * FF_CHUNK), lambda bi, t: (0, bi, 0, 0))
    return pl.pallas_call(
        kern,
        grid=grid,
        in_specs=[
            pl.BlockSpec((bB, tT, D_MODEL), tile),
            pl.BlockSpec((bB, tT, wa), tile),
            b_spec,
            _const_spec((wa + wb, D_MODEL)),
            _const_spec((1, D_MODEL)),
            _const_spec((1, D_MODEL)),
            _const_spec((N_FF_CHUNKS, D_MODEL, 2 * FF_CHUNK)),
            _const_spec((N_FF_CHUNKS, FFN_CONV, 2 * FF_CHUNK)),
            state,
            _const_spec((N_FF_CHUNKS, FF_CHUNK, D_MODEL)),
            _const_spec((1, D_MODEL)),
            _const_spec((D_MODEL, D_MODEL)),
            pl.BlockSpec((bB, tT, PLE_DIM), tile),
            _const_spec((PLE_DIM, D_MODEL)),
        ],
        out_specs=[pl.BlockSpec((bB, tT, D_MODEL), tile), state],
        out_shape=[
            jax.ShapeDtypeStruct((B, L, D_MODEL), F32),
            jax.ShapeDtypeStruct((N_FF_CHUNKS, B, HIST, 2 * FF_CHUNK), F32),
        ],
        scratch_shapes=[
            pltpu.VMEM((bB, tT + CONV_PAD, 2 * FF_CHUNK), F32),
            pltpu.VMEM((bB * tT, D_MODEL), F32),
        ],
        compiler_params=_params("arbitrary", "arbitrary"),
        name="mix_out_ffn",
    )(h, a, b, lw["w_out"], lw["g_post"], lw["g_ffn_pre"], lw["w_up"], lw["w_fconv"], fh, lw["w_down"],
      lw["g_ffn_post"], lw["w_gate"], pe, lw["w_ple"])


def _ff_cols_to_chunks(x):
    lead = x.shape[:-1]
    x = x.reshape(lead + (2, N_FF_CHUNKS, FF_CHUNK))
    x = jnp.moveaxis(x, -2, 0)
    return x.reshape((N_FF_CHUNKS,) + lead + (2 * FF_CHUNK,))


def _ff_chunks_to_cols(x):
    lead = x.shape[1:-1]
    x = x.reshape((N_FF_CHUNKS,) + lead + (2, FF_CHUNK))
    x = jnp.moveaxis(x, 0, -2)
    return x.reshape(lead + (2 * D_FF,))


def _s5_block_layout(x, state_major):
    per_quad = S5_PAIRS // S5_QUADS
    eye2 = jnp.eye(2, dtype=x.dtype)
    slot = jax.nn.one_hot(jnp.arange(S5_PAIRS) % per_quad, per_quad, dtype=x.dtype)
    if state_major:
        x4 = x.reshape(S5_PAIRS, 2, S5_STATE, S5_GROUP)
        blk = jnp.einsum("jgpc,gh->jgchp", x4, eye2).reshape(S5_PAIRS, 2 * S5_GROUP, LANES)
        return jnp.einsum("jkn,jr->jrkn", blk, slot).reshape(S5_PAIRS, LANES, LANES)
    x4 = x.reshape(S5_PAIRS, 2, S5_GROUP, S5_STATE)
    blk = jnp.einsum("jgcp,gh->jgphc", x4, eye2).reshape(S5_PAIRS, LANES, 2 * S5_GROUP)
    return jnp.einsum("jnk,jr->jnrk", blk, slot).reshape(S5_PAIRS, LANES, LANES)


def _layer_params(i, W, lc):
    j = i // 2
    lw = {
        "g_pre": W["g_mix_pre"][i][None], "g_post": W["g_mix_post"][i][None],
        "g_ffn_pre": W["g_ffn_pre"][i][None], "g_ffn_post": W["g_ffn_post"][i][None],
        "w_up": _ff_cols_to_chunks(W["w_ffn_up"][i]).astype(BF16),
        "w_fconv": _ff_cols_to_chunks(W["w_ffn_conv"][i]),
        "w_down": W["w_ffn_down"][i].reshape(N_FF_CHUNKS, FF_CHUNK, D_MODEL).astype(BF16),
        "w_gate": W["w_ple_gate"][i].astype(BF16), "w_ple": W["w_ple"][i].astype(BF16),
    }
    if i % 2 == 0:
        lw.update({
            "w_in": W["w_even_in"][j].astype(BF16), "w_conv": W["w_conv_a"][j],
            "w_out": W["w_even_out"][j].astype(BF16),
            "s5": {
                "a_re": W["s5_a_re"][j].reshape(S5_PAIRS, LANES),
                "a_im": W["s5_a_im"][j].reshape(S5_PAIRS, LANES),
                "ldt": jnp.broadcast_to(W["s5_log_dt"][j][:, None], (S5_GROUPS, S5_STATE)).reshape(S5_PAIRS, LANES),
                "b_re": _s5_block_layout(W["s5_b_re"][j], True), "b_im": _s5_block_layout(W["s5_b_im"][j], True),
                "c_re": _s5_block_layout(W["s5_c_re"][j], False), "c_im": _s5_block_layout(W["s5_c_im"][j], False),
                "d": W["s5_d"][j].reshape(1, B_WIDTH), "w_glu": W["w_glu"][j].astype(BF16),
            },
        })
    else:
        w = W["w_odd_in"][j]
        f0 = 3 * C_WIDTH
        wf = w[:, f0:f0 + C_HEADS]
        lw.update({
            "w_in": jnp.concatenate([w[:, :f0], w[:, f0 + C_HEADS:]], axis=1).astype(BF16),
            "w_f": wf.astype(BF16), "w_ft": wf.T.astype(BF16),
            "b_f": W["b_forget"][j][None], "b_ft": W["b_forget"][j][:, None],
            "g_v": W["g_gmlp_v"][j][None],
            "w_s": W["w_spatial"][j][:, :lc, :lc],
            "b_s": jnp.repeat(W["b_spatial"][j][:, :lc].T, D_WIDTH // C_HEADS, axis=1),
            "w_out": W["w_odd_out"][j].astype(BF16),
        })
    return lw


def _run_trunk(x, pe, st, W, bB, tT, tT_s5, tq):
    B, L, _ = x.shape
    prompt = st is None
    lc = min(L, GMLP_CHUNK)
    h = x
    conv_a, ssm_re, ssm_im, ks, vs, lfs, gvs, ffs = [], [], [], [], [], [], [], []
    for i in range(DEPTH):
        j = i // 2
        lw = _layer_params(i, W, lc)
        if i % 2 == 0:
            if prompt:
                hist = jnp.zeros((B, HIST, A_WIDTH), F32)
                h0r = jnp.zeros((B, S5_GROUPS * S5_STATE), F32)
                h0i = h0r
            else:
                hist = st["conv_a"][j]
                h0r = st["ssm_re"][j].reshape(B, S5_GROUPS * S5_STATE)
                h0i = st["ssm_im"][j].reshape(B, S5_GROUPS * S5_STATE)
            ya, u_tm, new_hist = _even_in(h, lw["g_pre"], lw["w_in"], lw["w_conv"], hist, bB, tT)
            yb_tm, nr, ni = _s5(u_tm, lw["s5"], h0r, h0i, B, L, tT_s5)
            a, b, b_tm = ya, yb_tm.reshape(L, B * B_WIDTH), True
            conv_a.append(new_hist)
            ssm_re.append(nr.reshape(B, S5_GROUPS, S5_STATE))
            ssm_im.append(ni.reshape(B, S5_GROUPS, S5_STATE))
        else:
            q, k, v, kb, vb, logf, fc, ft, yd, vn = _odd_in(
                h, lw["g_pre"], lw["w_in"], lw["w_f"], lw["w_ft"], lw["b_f"], lw["b_ft"], lw["g_v"],
                lw["w_s"], lw["b_s"], bB, tT)
            if prompt:
                att = _attn_prompt(q, kb, vb, fc, ft, tq, tq)
            else:
                P = st["k"].shape[2]
                att = _attn_sample(
                    q, kb, vb, st["k"][j].reshape(B, P, C_WIDTH), st["v"][j].reshape(B, P, C_WIDTH),
                    jnp.swapaxes(st["logf"][j], 1, 2), fc,
                    jnp.swapaxes(ft.reshape(C_HEADS, B, L), 0, 1))
            a, b, b_tm = att, yd, False
            ks.append(k.reshape(B, L, C_HEADS, C_HEAD_DIM))
            vs.append(v.reshape(B, L, C_HEADS, C_HEAD_DIM))
            lfs.append(logf)
            gvs.append(vn)
        fh = jnp.zeros((N_FF_CHUNKS, B, HIST, 2 * FF_CHUNK), F32) if prompt else _ff_cols_to_chunks(st["ffn"][i])
        h, new_f = _ffn(h, a, b, b_tm, lw, fh, pe[i], bB, tT)
        ffs.append(_ff_chunks_to_cols(new_f))
    new_state = {"conv_a": jnp.stack(conv_a), "ssm_re": jnp.stack(ssm_re), "ssm_im": jnp.stack(ssm_im),
                 "k": jnp.stack(ks), "v": jnp.stack(vs), "logf": jnp.stack(lfs),
                 "gmlp_v": jnp.stack(gvs), "ffn": jnp.stack(ffs)}
    return h, new_state


def kernel(x_prompt, x_sample, p_prompt, p_sample, cache_conv_a, state_ssm_re, state_ssm_im, cache_k, cache_v, cache_logf, cache_ffn_conv, g_mix_pre, g_mix_post, g_ffn_pre, g_ffn_post, w_even_in, w_conv_a, s5_a_re, s5_a_im, s5_log_dt, s5_b_re, s5_b_im, s5_c_re, s5_c_im, s5_d, w_glu, w_even_out, w_odd_in, b_forget, w_spatial, b_spatial, g_gmlp_v, w_odd_out, w_ffn_up, w_ffn_conv, w_ffn_down, w_ple, w_ple_gate):
    W = {"g_mix_pre": g_mix_pre, "g_mix_post": g_mix_post, "g_ffn_pre": g_ffn_pre, "g_ffn_post": g_ffn_post,
         "w_even_in": w_even_in, "w_conv_a": w_conv_a, "s5_a_re": s5_a_re, "s5_a_im": s5_a_im,
         "s5_log_dt": s5_log_dt, "s5_b_re": s5_b_re, "s5_b_im": s5_b_im, "s5_c_re": s5_c_re,
         "s5_c_im": s5_c_im, "s5_d": s5_d, "w_glu": w_glu, "w_even_out": w_even_out,
         "w_odd_in": w_odd_in, "b_forget": b_forget, "w_spatial": w_spatial, "b_spatial": b_spatial,
         "g_gmlp_v": g_gmlp_v, "w_odd_out": w_odd_out, "w_ffn_up": w_ffn_up, "w_ffn_conv": w_ffn_conv,
         "w_ffn_down": w_ffn_down, "w_ple": w_ple, "w_ple_gate": w_ple_gate}
    y_prompt, sp = _run_trunk(x_prompt, p_prompt, None, W, bB=1, tT=256, tT_s5=32, tq=256)
    st = {"conv_a": cache_conv_a, "ssm_re": state_ssm_re, "ssm_im": state_ssm_im,
          "k": cache_k, "v": cache_v, "logf": cache_logf, "ffn": cache_ffn_conv}
    dec_b, dec_l, _ = x_sample.shape
    y_sample, ss = _run_trunk(x_sample, p_sample, st, W, bB=dec_b, tT=dec_l, tT_s5=dec_l, tq=dec_l)
    return (y_prompt, y_sample,
            sp["conv_a"], sp["ssm_re"], sp["ssm_im"], sp["k"], sp["v"], sp["logf"], sp["ffn"],
            ss["conv_a"], ss["ssm_re"], ss["ssm_im"], ss["k"], ss["v"], ss["logf"], ss["gmlp_v"], ss["ffn"])
```

```python
import functools

import jax
import jax.numpy as jnp
import numpy as np
from jax import lax
from jax.experimental import pallas as pl
from jax.experimental.pallas import tpu as pltpu

F32 = jnp.float32
BF16 = jnp.bfloat16

D_MODEL = 1024
DEPTH = 4
N_ODD = DEPTH // 2
PLE_DIM = 256
RMS_EPS = 1e-6
A_WIDTH = 512
A_CONV = 3
B_WIDTH = 512
S5_GROUP = 16
S5_GROUPS = 32
S5_STATE = 64
S5_PAIRS = S5_GROUPS // 2
S5_QUADS = B_WIDTH // 128
C_HEADS = 8
C_HEAD_DIM = 64
C_WIDTH = 512
HEAD_PAIRS = C_HEADS // 2
D_WIDTH = 512
GMLP_CHUNK = 128
D_FF = 2816
FFN_CONV = 3
FF_CHUNK = 256
N_FF_CHUNKS = D_FF // FF_CHUNK
MASK_VALUE = -1e30
CONV_PAD = 8
HIST = A_CONV - 1

V7X_VMEM_LIMIT = 56 * 1024 * 1024
LANES = 128
LOG2E = 1.4426950408889634


def _rms(x, g):
    return x * lax.rsqrt(jnp.mean(x * x, axis=-1, keepdims=True) + RMS_EPS) * g


GELU_A = 0.7978845608028654
GELU_B = GELU_A * 0.044715


def _gelu_tanh(x):
    half = 0.5 * x
    return half + half * jnp.tanh(x * (GELU_A + GELU_B * (x * x)))


def _dot(a, b):
    return jnp.dot(a, b, preferred_element_type=F32)


def _dot_nt(a, b):
    return lax.dot_general(a, b, (((1,), (1,)), ((), ())), preferred_element_type=F32)


def _dot_exact(a, b):
    return jnp.dot(a, b, preferred_element_type=F32, precision=lax.Precision.HIGHEST)


def _const_spec(shape):
    zeros = (0,) * len(shape)
    return pl.BlockSpec(shape, lambda *_: zeros, pipeline_mode=pl.Buffered(1))


def _params(*sem):
    return pltpu.CompilerParams(dimension_semantics=sem, vmem_limit_bytes=V7X_VMEM_LIMIT)


def _causal_conv3(cb_ref, cur, hist, w, tT):
    cb_ref[:, CONV_PAD - HIST:CONV_PAD, :] = hist
    cb_ref[:, CONV_PAD:CONV_PAD + tT, :] = cur
    y = w[0:1][None] * cb_ref[:, CONV_PAD - 2:CONV_PAD - 2 + tT, :]
    y = y + w[1:2][None] * cb_ref[:, CONV_PAD - 1:CONV_PAD - 1 + tT, :]
    y = y + w[2:3][None] * cur
    return y, cb_ref[:, CONV_PAD + tT - HIST:CONV_PAD + tT, :]


def _even_in_kernel(h_ref, g_ref, w_ref, wc_ref, hist_ref, ya_ref, u_ref, nh_ref, cb_ref, *, bB, tT):
    @pl.when(pl.program_id(1) == 0)
    def _():
        nh_ref[...] = hist_ref[...]

    x = h_ref[...].reshape(bB * tT, D_MODEL)
    hn = _rms(x, g_ref[...])
    z = _dot(hn.astype(BF16), w_ref[...])
    gb = z[:, 0:A_WIDTH]
    gc = z[:, A_WIDTH:2 * A_WIDTH]
    xa = z[:, 2 * A_WIDTH:3 * A_WIDTH]
    u = z[:, 3 * A_WIDTH:]
    cx = (gc * xa).reshape(bB, tT, A_WIDTH)
    yc, last = _causal_conv3(cb_ref, cx, nh_ref[...], wc_ref[...], tT)
    nh_ref[...] = last
    ya_ref[...] = (gb.reshape(bB, tT, A_WIDTH) * yc).astype(BF16)
    for b in range(bB):
        u_ref[:, b * B_WIDTH:(b + 1) * B_WIDTH] = u[b * tT:(b + 1) * tT]


def _even_in(h, g, w, wc, hist, bB, tT):
    B, L, _ = h.shape
    grid = (B // bB, L // tT)
    kern = functools.partial(_even_in_kernel, bB=bB, tT=tT)
    return pl.pallas_call(
        kern,
        grid=grid,
        in_specs=[
            pl.BlockSpec((bB, tT, D_MODEL), lambda b, t: (b, t, 0)),
            _const_spec((1, D_MODEL)),
            _const_spec((D_MODEL, 4 * A_WIDTH)),
            _const_spec((A_CONV, A_WIDTH)),
            pl.BlockSpec((bB, HIST, A_WIDTH), lambda b, t: (b, 0, 0)),
        ],
        out_specs=[
            pl.BlockSpec((bB, tT, A_WIDTH), lambda b, t: (b, t, 0)),
            pl.BlockSpec((tT, bB * B_WIDTH), lambda b, t: (t, b)),
            pl.BlockSpec((bB, HIST, A_WIDTH), lambda b, t: (b, 0, 0)),
        ],
        out_shape=[
            jax.ShapeDtypeStruct((B, L, A_WIDTH), BF16),
            jax.ShapeDtypeStruct((L, B * B_WIDTH), F32),
            jax.ShapeDtypeStruct((B, HIST, A_WIDTH), F32),
        ],
        scratch_shapes=[pltpu.VMEM((bB, tT + CONV_PAD, A_WIDTH), F32)],
        compiler_params=_params("arbitrary", "arbitrary"),
        name="even_in",
    )(h, g, w, wc, hist)


S5_SCAN_PAIRS = 4


def _s5_kernel(u_ref, are_ref, aim_ref, ldt_ref, bre_ref, bim_ref, cre_ref, cim_ref, d_ref, wglu_ref,
               h0r_ref, h0i_ref, yb_ref, hr_ref, hi_ref, lam_ref, wb_ref, wc_ref, sre_ref, sim_ref, *, B, tT):
    M = tT * B

    @pl.when(pl.program_id(0) == 0)
    def _():
        a_re = are_ref[...]
        a_im = aim_ref[...]
        dt = jnp.exp(ldt_ref[...])
        mag = jnp.exp(a_re * dt)
        ab_re = mag * jnp.cos(a_im * dt)
        ab_im = mag * jnp.sin(a_im * dt)
        z_re = ab_re - 1.0
        den = a_re * a_re + a_im * a_im
        f_re = (z_re * a_re + ab_im * a_im) / den
        f_im = (ab_im * a_re - z_re * a_im) / den
        lam_ref[0] = ab_re
        lam_ref[1] = ab_im
        for j in range(S5_PAIRS):
            fr = f_re[j:j + 1]
            fi = f_im[j:j + 1]
            br = bre_ref[j]
            bi = bim_ref[j]
            wb_ref[j, :, 0:LANES] = (fr * br - fi * bi).astype(BF16)
            wb_ref[j, :, LANES:2 * LANES] = (fr * bi + fi * br).astype(BF16)
            wc_ref[j, 0:LANES, :] = cre_ref[j].astype(BF16)
            wc_ref[j, LANES:2 * LANES, :] = (-cim_ref[j]).astype(BF16)
        hr_ref[...] = h0r_ref[...]
        hi_ref[...] = h0i_ref[...]

    u = u_ref[...].reshape(M, B_WIDTH)
    ub = u.astype(BF16)
    for j in range(S5_PAIRS):
        q = j // (S5_PAIRS // S5_QUADS)
        bu = _dot(ub[:, q * LANES:(q + 1) * LANES], wb_ref[j])
        sre_ref[j] = bu[:, 0:LANES]
        sim_ref[j] = bu[:, LANES:2 * LANES]

    for j0 in range(0, S5_PAIRS, S5_SCAN_PAIRS):
        pairs = range(j0, j0 + S5_SCAN_PAIRS)
        lr = [lam_ref[0, j:j + 1, :] for j in pairs]
        li = [lam_ref[1, j:j + 1, :] for j in pairs]
        hr0 = tuple(hr_ref[:, j * LANES:(j + 1) * LANES] for j in pairs)
        hi0 = tuple(hi_ref[:, j * LANES:(j + 1) * LANES] for j in pairs)

        def step(t, carry, lr=lr, li=li, pairs=pairs):
            hr, hi = carry
            row = pl.multiple_of(t * B, B)
            nr, ni = [], []
            for k, j in enumerate(pairs):
                br = sre_ref[j, pl.ds(row, B), :]
                bi = sim_ref[j, pl.ds(row, B), :]
                r = lr[k] * hr[k] - li[k] * hi[k] + br
                i = lr[k] * hi[k] + li[k] * hr[k] + bi
                sre_ref[j, pl.ds(row, B), :] = r
                sim_ref[j, pl.ds(row, B), :] = i
                nr.append(r)
                ni.append(i)
            return tuple(nr), tuple(ni)

        hr, hi = lax.fori_loop(0, tT, step, (hr0, hi0))
        for k, j in enumerate(pairs):
            hr_ref[:, j * LANES:(j + 1) * LANES] = hr[k]
            hi_ref[:, j * LANES:(j + 1) * LANES] = hi[k]

    d = d_ref[...]
    cols = []
    per_quad = S5_PAIRS // S5_QUADS
    for q in range(S5_QUADS):
        acc = None
        for j in range(q * per_quad, (q + 1) * per_quad):
            hs = jnp.concatenate([sre_ref[j], sim_ref[j]], axis=-1).astype(BF16)
            part = _dot(hs, wc_ref[j])
            acc = part if acc is None else acc + part
        cols.append(acc + d[:, q * LANES:(q + 1) * LANES] * u[:, q * LANES:(q + 1) * LANES])
    y = jnp.concatenate(cols, axis=-1)
    g = _gelu_tanh(y)
    out = g * jax.nn.sigmoid(_dot(g.astype(BF16), wglu_ref[...]))
    yb_ref[...] = out.reshape(tT, B, B_WIDTH)


def _s5(u_tm, prm, h0r, h0i, B, L, tT):
    M = tT * B
    state = S5_PAIRS * LANES
    kern = functools.partial(_s5_kernel, B=B, tT=tT)
    tile = lambda i: (i, 0, 0)
    return pl.pallas_call(
        kern,
        grid=(L // tT,),
        in_specs=[
            pl.BlockSpec((tT, B, B_WIDTH), tile),
            _const_spec((S5_PAIRS, LANES)),
            _const_spec((S5_PAIRS, LANES)),
            _const_spec((S5_PAIRS, LANES)),
            _const_spec((S5_PAIRS, LANES, LANES)),
            _const_spec((S5_PAIRS, LANES, LANES)),
            _const_spec((S5_PAIRS, LANES, LANES)),
            _const_spec((S5_PAIRS, LANES, LANES)),
            _const_spec((1, B_WIDTH)),
            _const_spec((B_WIDTH, B_WIDTH)),
            _const_spec((B, state)),
            _const_spec((B, state)),
        ],
        out_specs=[
            pl.BlockSpec((tT, B, B_WIDTH), tile),
            pl.BlockSpec((B, state), lambda i: (0, 0)),
            pl.BlockSpec((B, state), lambda i: (0, 0)),
        ],
        out_shape=[
            jax.ShapeDtypeStruct((L, B, B_WIDTH), F32),
            jax.ShapeDtypeStruct((B, state), F32),
            jax.ShapeDtypeStruct((B, state), F32),
        ],
        scratch_shapes=[
            pltpu.VMEM((2, S5_PAIRS, LANES), F32),
            pltpu.VMEM((S5_PAIRS, LANES, 2 * LANES), BF16),
            pltpu.VMEM((S5_PAIRS, 2 * LANES, LANES), BF16),
            pltpu.VMEM((S5_PAIRS, M, LANES), F32),
            pltpu.VMEM((S5_PAIRS, M, LANES), F32),
        ],
        compiler_params=_params("arbitrary"),
        name="s5_scan",
    )(u_tm.reshape(L, B, B_WIDTH), prm["a_re"], prm["a_im"], prm["ldt"], prm["b_re"], prm["b_im"],
      prm["c_re"], prm["c_im"], prm["d"], prm["w_glu"], h0r, h0i)


def _odd_in_kernel(*refs, bB, tT, lc, kv_channel_major, n_alias):
    (h_ref, g_ref, w_ref, wkv_ref, wft_ref, bft_ref, gv_ref, ws_ref, bs_ref) = refs[:9]
    outs = refs[9 + n_alias:]
    if kv_channel_major:
        q_ref, k_ref, v_ref, kb_ref, vb_ref, lf_ref, ft_ref, yd_ref, carry_ref = outs
        vn_ref = None
    else:
        q_ref, k_ref, v_ref, kb_ref, vb_ref, lf_ref, ft_ref, yd_ref, vn_ref, carry_ref = outs
    M = bB * tT

    @pl.when(pl.program_id(1) == 0)
    def _():
        carry_ref[...] = jnp.zeros_like(carry_ref)

    x = h_ref[...].reshape(M, D_MODEL)
    hn = _rms(x, g_ref[...]).astype(BF16)
    z = _dot(hn, w_ref[...])
    q = z[:, 0:C_WIDTH] * (C_HEAD_DIM ** -0.5 * LOG2E)
    u = z[:, C_WIDTH:C_WIDTH + D_WIDTH]
    vd = z[:, C_WIDTH + D_WIDTH:C_WIDTH + 2 * D_WIDTH]
    q_ref[...] = q.reshape(bB, tT, C_WIDTH).astype(BF16)
    if kv_channel_major:
        kv = _dot_nt(wkv_ref[...], hn)
        k_ref[0, 0] = kv[0:C_WIDTH]
        v_ref[0, 0] = kv[C_WIDTH:]
        kb_ref[0] = kv[0:C_WIDTH].astype(BF16)
        ones = jnp.ones((C_HEAD_DIM, M), BF16)
        for hh in range(C_HEADS):
            v_at = hh * LANES + (hh % 2) * C_HEAD_DIM
            ones_at = hh * LANES + (1 - hh % 2) * C_HEAD_DIM
            vb_ref[0, v_at:v_at + C_HEAD_DIM, :] = (
                kv[C_WIDTH + hh * C_HEAD_DIM:C_WIDTH + (hh + 1) * C_HEAD_DIM].astype(BF16))
            vb_ref[0, ones_at:ones_at + C_HEAD_DIM, :] = ones
    else:
        k = z[:, C_WIDTH + 2 * D_WIDTH:2 * C_WIDTH + 2 * D_WIDTH].reshape(bB, tT, C_WIDTH)
        v = z[:, 2 * C_WIDTH + 2 * D_WIDTH:].reshape(bB, tT, C_WIDTH)
        k_ref[...] = k
        v_ref[...] = v
        kb_ref[...] = k.astype(BF16)
        vb_ref[...] = v.astype(BF16)

    logf_t = jax.nn.log_sigmoid(_dot_nt(wft_ref[...], hn) + bft_ref[...])
    r = lax.broadcasted_iota(jnp.int32, (M, M), 0)
    c = lax.broadcasted_iota(jnp.int32, (M, M), 1)
    upper = jnp.where(((r // tT) == (c // tT)) & (r <= c), 1.0, 0.0).astype(F32)
    ft = _dot_exact(logf_t, upper) + carry_ref[...]
    for b in range(bB):
        lf_ref[b] = logf_t[:, b * tT:(b + 1) * tT]
        ft_ref[b] = ft[:, b * tT:(b + 1) * tT]
    if bB == 1:
        carry_ref[...] = ft[:, M - 1:M]

    vn = _rms(vd, gv_ref[...])
    if vn_ref is not None:
        vn_ref[...] = vn.reshape(bB, tT, D_WIDTH)
    vnb = vn.astype(BF16)
    rr = lax.broadcasted_iota(jnp.int32, (lc, lc), 0)
    cc = lax.broadcasted_iota(jnp.int32, (lc, lc), 1)
    wsm = [jnp.where(cc <= rr, ws_ref[hh], 0.0).astype(BF16) for hh in range(C_HEADS)]
    first_head = lax.broadcasted_iota(jnp.int32, (lc, LANES), 1) < C_HEAD_DIM
    bias = bs_ref[...]
    for ch in range(M // lc):
        rows = slice(ch * lc, (ch + 1) * lc)
        cols = []
        for hp in range(HEAD_PAIRS):
            vp = vnb[rows, hp * LANES:(hp + 1) * LANES]
            cols.append(jnp.where(first_head, _dot(wsm[2 * hp], vp), _dot(wsm[2 * hp + 1], vp)))
        mixed = jnp.concatenate(cols, axis=-1) + bias
        b, off = divmod(ch * lc, tT)
        yd_ref[b, off:off + lc, :] = (u[rows] * mixed).astype(BF16)


def _odd_in(h, lw, bB, tT, j, kv_stacks):
    B, L, _ = h.shape
    nT = L // tT
    assert bB == 1 or nT == 1
    kv_channel_major = kv_stacks is not None
    lc = lw["w_s"].shape[-1]
    tile = lambda b, t: (b, t, 0)
    chan = lambda b, t: (b, 0, t)

    def act(width, dtype):
        return pl.BlockSpec((bB, tT, width), tile), jax.ShapeDtypeStruct((B, L, width), dtype)

    heads = (pl.BlockSpec((bB, C_HEADS, tT), chan), jax.ShapeDtypeStruct((B, C_HEADS, L), F32))
    if kv_channel_major:
        assert bB == 1
        stack = (pl.BlockSpec((1, 1, C_WIDTH, tT), lambda b, t: (j, b, 0, t)),
                 jax.ShapeDtypeStruct((N_ODD, B, C_WIDTH, L), F32))
        chan_bf = (pl.BlockSpec((1, C_WIDTH, tT), chan), jax.ShapeDtypeStruct((B, C_WIDTH, L), BF16))
        chan_v = (pl.BlockSpec((1, C_HEADS * LANES, tT), chan), jax.ShapeDtypeStruct((B, C_HEADS * LANES, L), BF16))
        outs = [act(C_WIDTH, BF16), stack, stack, chan_bf, chan_v, heads, heads, act(D_WIDTH, BF16)]
        w_kv_spec = _const_spec((2 * C_WIDTH, D_MODEL))
    else:
        outs = [act(C_WIDTH, BF16), act(C_WIDTH, F32), act(C_WIDTH, F32), act(C_WIDTH, BF16), act(C_WIDTH, BF16),
                heads, heads, act(D_WIDTH, BF16), act(D_WIDTH, F32)]
        w_kv_spec = _const_spec((1, 1))
    alias_in = list(kv_stacks) if kv_channel_major else []
    n_fixed = 9
    kern = functools.partial(_odd_in_kernel, bB=bB, tT=tT, lc=lc, kv_channel_major=kv_channel_major,
                             n_alias=len(alias_in))
    w_in = lw["w_in_cm"] if kv_channel_major else lw["w_in_rm"]
    w_kv = lw["w_kv_t"] if kv_channel_major else jnp.zeros((1, 1), BF16)
    return pl.pallas_call(
        kern,
        grid=(B // bB, nT),
        in_specs=[
            pl.BlockSpec((bB, tT, D_MODEL), tile),
            _const_spec((1, D_MODEL)),
            _const_spec(w_in.shape),
            w_kv_spec,
            _const_spec((C_HEADS, D_MODEL)),
            _const_spec((C_HEADS, 1)),
            _const_spec((1, D_WIDTH)),
            _const_spec((C_HEADS, lc, lc)),
            _const_spec((lc, D_WIDTH)),
        ] + [pl.BlockSpec(memory_space=pl.ANY)] * len(alias_in),
        out_specs=[o[0] for o in outs],
        out_shape=[o[1] for o in outs],
        input_output_aliases={n_fixed + i: 1 + i for i in range(len(alias_in))},
        scratch_shapes=[pltpu.VMEM((C_HEADS, 1), F32)],
        compiler_params=_params("arbitrary", "arbitrary"),
        name="odd_in",
    )(h, lw["g_pre"], w_in, w_kv, lw["w_ft"], lw["b_ft"], lw["g_v"], lw["w_s"], lw["b_s"], *alias_in)


def _attn_prompt_kernel(qi_ref, kj_ref, q_ref, kt_ref, vt_ref, ft_ref, o_ref, m_ref, acc_ref, *, tq):
    p = pl.program_id(1)
    qi = qi_ref[p]
    kj = kj_ref[p]
    first_head = lax.broadcasted_iota(jnp.int32, (tq, LANES), 1) < C_HEAD_DIM

    @pl.when(kj == 0)
    def _():
        m_ref[...] = jnp.full_like(m_ref, -jnp.inf)
        acc_ref[...] = jnp.zeros_like(acc_ref)

    def block(diagonal):
        if diagonal:
            visible = (lax.broadcasted_iota(jnp.int32, (tq, tq), 1)
                       <= lax.broadcasted_iota(jnp.int32, (tq, tq), 0))
        fk = ft_ref[0] * LOG2E
        for hp in range(HEAD_PAIRS):
            lanes = slice(hp * LANES, (hp + 1) * LANES)
            qp = q_ref[0, :, lanes]
            kt = kt_ref[0, lanes, :]
            qs = (jnp.where(first_head, qp, jnp.zeros_like(qp)), jnp.where(first_head, jnp.zeros_like(qp), qp))
            done = []
            for e in range(2):
                h = 2 * hp + e
                s = _dot(qs[e], kt) - fk[h:h + 1]
                if diagonal:
                    s = jnp.where(visible, s, MASK_VALUE)
                m_old = m_ref[h]
                m_new = jnp.maximum(m_old, jnp.max(s, axis=-1, keepdims=True))
                a = jnp.exp2(m_old - m_new)
                pr = jnp.exp2(s - jnp.concatenate([m_new] * (tq // LANES), axis=-1))
                acc = acc_ref[h] * a + _dot_nt(pr.astype(BF16), vt_ref[0, h * LANES:(h + 1) * LANES, :])
                if diagonal:
                    done.append(acc)
                else:
                    m_ref[h] = m_new
                    acc_ref[h] = acc
            if diagonal:
                num = jnp.where(first_head, done[0], done[1])
                den = pltpu.roll(jnp.where(first_head, done[1], done[0]), C_HEAD_DIM, axis=1)
                o_ref[0, :, lanes] = (num / den).astype(BF16)

    @pl.when(kj < qi)
    def _():
        block(False)

    @pl.when(kj == qi)
    def _():
        block(True)


def _attn_prompt(q, ktb, vtb, ft, tq):
    B, L, _ = q.shape
    nq = L // tq
    pairs = [(i, k) for i in range(nq) for k in range(i + 1)]
    qi = jnp.asarray(np.array([p[0] for p in pairs], np.int32))
    kj = jnp.asarray(np.array([p[1] for p in pairs], np.int32))
    kern = functools.partial(_attn_prompt_kernel, tq=tq)
    q_spec = pl.BlockSpec((1, tq, C_WIDTH), lambda b, p, qi, kj: (b, qi[p], 0))
    return pl.pallas_call(
        kern,
        grid_spec=pltpu.PrefetchScalarGridSpec(
            num_scalar_prefetch=2,
            grid=(B, len(pairs)),
            in_specs=[q_spec,
                      pl.BlockSpec((1, C_WIDTH, tq), lambda b, p, qi, kj: (b, 0, kj[p])),
                      pl.BlockSpec((1, C_HEADS * LANES, tq), lambda b, p, qi, kj: (b, 0, kj[p])),
                      pl.BlockSpec((1, C_HEADS, tq), lambda b, p, qi, kj: (b, 0, kj[p]))],
            out_specs=q_spec,
            scratch_shapes=[pltpu.VMEM((C_HEADS, tq, LANES), F32), pltpu.VMEM((C_HEADS, tq, LANES), F32)],
        ),
        out_shape=jax.ShapeDtypeStruct((B, L, C_WIDTH), BF16),
        compiler_params=_params("arbitrary", "arbitrary"),
        name="attn_prompt",
    )(qi, kj, q, ktb, vtb, ft)


def _attn_sample_kernel(q_ref, kn_ref, vn_ref, kpt_ref, vpt_ref, lpt_ref, ft_ref, o_ref, *, L, P):
    first_head = lax.broadcasted_iota(jnp.int32, (L, LANES), 1) < C_HEAD_DIM
    r = lax.broadcasted_iota(jnp.int32, (P, P), 0)
    c = lax.broadcasted_iota(jnp.int32, (P, P), 1)
    upper = jnp.where(r <= c, 1.0, 0.0).astype(F32)
    past_cum = _dot_exact(lpt_ref[0, 0], upper)
    past_after = (past_cum[:, P - 1:P] - past_cum) * LOG2E
    causal = lax.broadcasted_iota(jnp.int32, (L, L), 1) <= lax.broadcasted_iota(jnp.int32, (L, L), 0)
    ft = ft_ref[0] * LOG2E
    for hp in range(HEAD_PAIRS):
        lanes = slice(hp * LANES, (hp + 1) * LANES)
        qp = q_ref[0, :, lanes]
        qs = (jnp.where(first_head, qp, jnp.zeros_like(qp)), jnp.where(first_head, jnp.zeros_like(qp), qp))
        kpt = kpt_ref[0, 0, lanes, :].astype(BF16)
        vpt = vpt_ref[0, 0, lanes, :].astype(BF16)
        kn = kn_ref[0, :, lanes]
        vn = vn_ref[0, :, lanes]
        outs = []
        for e in range(2):
            h = 2 * hp + e
            sp = _dot(qs[e], kpt) + past_after[h:h + 1]
            sn = jnp.where(causal, _dot_nt(qs[e], kn) - ft[h:h + 1], MASK_VALUE)
            m = jnp.maximum(jnp.max(sp, axis=-1, keepdims=True), jnp.max(sn, axis=-1, keepdims=True))
            pp = jnp.exp2(sp - m)
            pn = jnp.exp2(sn - m)
            den = jnp.sum(pp, axis=-1, keepdims=True) + jnp.sum(pn, axis=-1, keepdims=True)
            outs.append((_dot_nt(pp.astype(BF16), vpt) + _dot(pn.astype(BF16), vn)) / den)
        o_ref[0, :, lanes] = jnp.where(first_head, outs[0], outs[1]).astype(BF16)


def _attn_sample(q, kb, vb, k_past_t, v_past_t, logf_past_t, ft, j):
    B, L, _ = q.shape
    P = k_past_t.shape[-1]
    kern = functools.partial(_attn_sample_kernel, L=L, P=P)
    new = pl.BlockSpec((1, L, C_WIDTH), lambda b: (b, 0, 0))
    past = pl.BlockSpec((1, 1, C_WIDTH, P), lambda b: (j, b, 0, 0))
    return pl.pallas_call(
        kern,
        grid=(B,),
        in_specs=[new, new, new, past, past,
                  pl.BlockSpec((1, 1, C_HEADS, P), lambda b: (j, b, 0, 0)),
                  pl.BlockSpec((1, C_HEADS, L), lambda b: (b, 0, 0))],
        out_specs=new,
        out_shape=jax.ShapeDtypeStruct((B, L, C_WIDTH), BF16),
        compiler_params=_params("arbitrary"),
        name="attn_sample",
    )(q, kb, vb, k_past_t, v_past_t, logf_past_t, ft)


def _ffn_kernel(h_ref, a_ref, b_ref, wo_ref, gpost_ref, gpre_ref, wup_ref, wcv_ref, fh_ref, wdn_ref, gfp_ref,
                wgate_ref, pe_ref, wple_ref, o_ref, nf_ref, cb_ref, acc_ref, *, bB, tT, b_time_major):
    M = bB * tT

    @pl.when(pl.program_id(1) == 0)
    def _():
        nf_ref[...] = fh_ref[0]

    a = a_ref[...].reshape(M, a_ref.shape[-1])
    if b_time_major:
        width = b_ref.shape[-1] // bB
        bpart = jnp.concatenate([b_ref[:, i * width:(i + 1) * width] for i in range(bB)], axis=0)
    else:
        bpart = b_ref[...].reshape(M, b_ref.shape[-1])
    ka = a.shape[-1]
    y = _dot(a.astype(BF16), wo_ref[0:ka, :]) + _dot(bpart.astype(BF16), wo_ref[ka:, :])
    h1 = h_ref[...].reshape(M, D_MODEL) + _rms(y, gpost_ref[...])
    hn = _rms(h1, gpre_ref[...]).astype(BF16)

    for c in range(N_FF_CHUNKS):
        cb = cb_ref.at[c % 2]
        halves = []
        for col0 in (c * FF_CHUNK, D_FF + c * FF_CHUNK):
            cols = slice(col0, col0 + FF_CHUNK)
            up = _dot(hn, wup_ref[:, cols]).reshape(bB, tT, FF_CHUNK)
            uc, last = _causal_conv3(cb.at[:, :, col0 // D_FF * FF_CHUNK:(col0 // D_FF + 1) * FF_CHUNK],
                                     up, nf_ref[:, :, cols], wcv_ref[:, cols], tT)
            nf_ref[:, :, cols] = last
            halves.append(uc)
        act = (_gelu_tanh(halves[0]) * halves[1]).reshape(M, FF_CHUNK).astype(BF16)
        part = _dot(act, wdn_ref[c * FF_CHUNK:(c + 1) * FF_CHUNK, :])
        if c == 0:
            acc_ref[...] = part
        else:
            acc_ref[...] += part

    h2 = h1 + _rms(acc_ref[...], gfp_ref[...])
    gate = jax.nn.sigmoid(_dot(h2.astype(BF16), wgate_ref[...]))
    pe = pe_ref[0].reshape(M, PLE_DIM).astype(BF16)
    h3 = h2 + gate * _dot(pe, wple_ref[...])
    o_ref[...] = h3.reshape(bB, tT, D_MODEL)


def _ffn(h, a, b, b_time_major, lw, fh, pe, layer, bB, tT):
    B, L, _ = h.shape
    grid = (B // bB, L // tT)
    kern = functools.partial(_ffn_kernel, bB=bB, tT=tT, b_time_major=b_time_major)
    tile = lambda bi, t: (bi, t, 0)
    wa = a.shape[-1]
    if b_time_major:
        wb = b.shape[-1] // B
        b_spec = pl.BlockSpec((tT, bB * wb), lambda bi, t: (t, bi))
    else:
        wb = b.shape[-1]
        b_spec = pl.BlockSpec((bB, tT, wb), tile)
    fslot = layer if fh.shape[0] > 1 else 0
    return pl.pallas_call(
        kern,
        grid=grid,
        in_specs=[
            pl.BlockSpec((bB, tT, D_MODEL), tile),
            pl.BlockSpec((bB, tT, wa), tile),
            b_spec,
            _const_spec((wa + wb, D_MODEL)),
            _const_spec((1, D_MODEL)),
            _const_spec((1, D_MODEL)),
            _const_spec((D_MODEL, 2 * D_FF)),
            _const_spec((FFN_CONV, 2 * D_FF)),
            pl.BlockSpec((1, bB, HIST, 2 * D_FF), lambda bi, t: (fslot, bi, 0, 0)),
            _const_spec((D_FF, D_MODEL)),
            _const_spec((1, D_MODEL)),
            _const_spec((D_MODEL, D_MODEL)),
            pl.BlockSpec((1, bB, tT, PLE_DIM), lambda bi, t: (layer, bi, t, 0)),
            _const_spec((PLE_DIM, D_MODEL)),
        ],
        out_specs=[pl.BlockSpec((bB, tT, D_MODEL), tile),
                   pl.BlockSpec((bB, HIST, 2 * D_FF), lambda bi, t: (bi, 0, 0))],
        out_shape=[
            jax.ShapeDtypeStruct((B, L, D_MODEL), F32),
            jax.ShapeDtypeStruct((B, HIST, 2 * D_FF), F32),
        ],
        scratch_shapes=[
            pltpu.VMEM((2, bB, tT + CONV_PAD, 2 * FF_CHUNK), F32),
            pltpu.VMEM((bB * tT, D_MODEL), F32),
        ],
        compiler_params=_params("arbitrary", "arbitrary"),
        name="mix_out_ffn",
    )(h, a, b, lw["w_out"], lw["g_post"], lw["g_ffn_pre"], lw["w_up"], lw["w_fconv"], fh, lw["w_down"],
      lw["g_ffn_post"], lw["w_gate"], pe, lw["w_ple"])


def _s5_block_layout(x, state_major):
    per_quad = S5_PAIRS // S5_QUADS
    eye2 = jnp.eye(2, dtype=x.dtype)
    slot = jax.nn.one_hot(jnp.arange(S5_PAIRS) % per_quad, per_quad, dtype=x.dtype)
    if state_major:
        x4 = x.reshape(S5_PAIRS, 2, S5_STATE, S5_GROUP)
        blk = jnp.einsum("jgpc,gh->jgchp", x4, eye2).reshape(S5_PAIRS, 2 * S5_GROUP, LANES)
        return jnp.einsum("jkn,jr->jrkn", blk, slot).reshape(S5_PAIRS, LANES, LANES)
    x4 = x.reshape(S5_PAIRS, 2, S5_GROUP, S5_STATE)
    blk = jnp.einsum("jgcp,gh->jgphc", x4, eye2).reshape(S5_PAIRS, LANES, 2 * S5_GROUP)
    return jnp.einsum("jnk,jr->jnrk", blk, slot).reshape(S5_PAIRS, LANES, LANES)


def _layer_params(i, W):
    j = i // 2
    lw = {
        "g_pre": W["g_mix_pre"][i][None], "g_post": W["g_mix_post"][i][None],
        "g_ffn_pre": W["g_ffn_pre"][i][None], "g_ffn_post": W["g_ffn_post"][i][None],
        "w_up": W["w_ffn_up"][i].astype(BF16), "w_fconv": W["w_ffn_conv"][i],
        "w_down": W["w_ffn_down"][i].astype(BF16),
        "w_gate": W["w_ple_gate"][i].astype(BF16), "w_ple": W["w_ple"][i].astype(BF16),
    }
    if i % 2 == 0:
        lw.update({
            "w_in": W["w_even_in"][j].astype(BF16), "w_conv": W["w_conv_a"][j],
            "w_out": W["w_even_out"][j].astype(BF16),
            "s5": {
                "a_re": W["s5_a_re"][j].reshape(S5_PAIRS, LANES),
                "a_im": W["s5_a_im"][j].reshape(S5_PAIRS, LANES),
                "ldt": jnp.broadcast_to(W["s5_log_dt"][j][:, None], (S5_GROUPS, S5_STATE)).reshape(S5_PAIRS, LANES),
                "b_re": _s5_block_layout(W["s5_b_re"][j], True), "b_im": _s5_block_layout(W["s5_b_im"][j], True),
                "c_re": _s5_block_layout(W["s5_c_re"][j], False), "c_im": _s5_block_layout(W["s5_c_im"][j], False),
                "d": W["s5_d"][j].reshape(1, B_WIDTH), "w_glu": W["w_glu"][j].astype(BF16),
            },
        })
    else:
        w = W["w_odd_in"][j].astype(BF16)
        f0 = 3 * C_WIDTH
        w_q, w_kv, w_f, w_ud = w[:, :C_WIDTH], w[:, C_WIDTH:f0], w[:, f0:f0 + C_HEADS], w[:, f0 + C_HEADS:]
        lw.update({
            "w_in_cm": jnp.concatenate([w_q, w_ud], axis=1),
            "w_in_rm": jnp.concatenate([w_q, w_ud, w_kv], axis=1),
            "w_kv_t": w_kv.T, "w_ft": w_f.T, "b_ft": W["b_forget"][j][:, None],
            "g_v": W["g_gmlp_v"][j][None],
            "w_out": W["w_odd_out"][j].astype(BF16),
        })
        for lc in (GMLP_CHUNK, 32):
            lw[("w_s", lc)] = W["w_spatial"][j][:, :lc, :lc]
            lw[("b_s", lc)] = jnp.repeat(W["b_spatial"][j][:, :lc].T, D_WIDTH // C_HEADS, axis=1)
    return lw


def _run_trunk(x, pe, st, LW, bB, tT, tT_s5, tq, tT_ffn):
    B, L, _ = x.shape
    prompt = st is None
    lc = min(L, GMLP_CHUNK)
    h = x
    conv_a, ssm_re, ssm_im, ks, vs, lfs, gvs, ffs = [], [], [], [], [], [], [], []
    kv_stacks = ()
    fh = jnp.zeros((1, B, HIST, 2 * D_FF), F32) if prompt else st["ffn"]
    for i in range(DEPTH):
        j = i // 2
        lw = LW[i]
        if i % 2 == 0:
            if prompt:
                hist = jnp.zeros((B, HIST, A_WIDTH), F32)
                h0r = jnp.zeros((B, S5_GROUPS * S5_STATE), F32)
                h0i = h0r
            else:
                hist = st["conv_a"][j]
                h0r = st["ssm_re"][j].reshape(B, S5_GROUPS * S5_STATE)
                h0i = st["ssm_im"][j].reshape(B, S5_GROUPS * S5_STATE)
            ya, u_tm, new_hist = _even_in(h, lw["g_pre"], lw["w_in"], lw["w_conv"], hist, bB, tT)
            yb_tm, nr, ni = _s5(u_tm, lw["s5"], h0r, h0i, B, L, tT_s5)
            a, b, b_tm = ya, yb_tm.reshape(L, B * B_WIDTH), True
            conv_a.append(new_hist)
            ssm_re.append(nr.reshape(B, S5_GROUPS, S5_STATE))
            ssm_im.append(ni.reshape(B, S5_GROUPS, S5_STATE))
        else:
            lw = dict(lw, w_s=lw[("w_s", lc)], b_s=lw[("b_s", lc)])
            if prompt:
                q, k_stack, v_stack, ktb, vtb, logf_t, ft, yd = _odd_in(h, lw, bB, tT, j, kv_stacks)
                kv_stacks = (k_stack, v_stack)
                att = _attn_prompt(q, ktb, vtb, ft, tq)
            else:
                q, k, v, kb, vb, logf_t, ft, yd, vn = _odd_in(h, lw, bB, tT, j, None)
                att = _attn_sample(q, kb, vb, st["k_t"], st["v_t"], st["logf_t"], ft, j)
                ks.append(k.reshape(B, L, C_HEADS, C_HEAD_DIM))
                vs.append(v.reshape(B, L, C_HEADS, C_HEAD_DIM))
                gvs.append(vn)
            a, b, b_tm = att, yd, False
            lfs.append(logf_t)
        h, new_f = _ffn(h, a, b, b_tm, lw, fh, pe, i, bB, tT_ffn)
        ffs.append(new_f)
    new_state = {"conv_a": jnp.stack(conv_a), "ssm_re": jnp.stack(ssm_re), "ssm_im": jnp.stack(ssm_im),
                 "logf": jnp.swapaxes(jnp.stack(lfs), 2, 3), "ffn": jnp.stack(ffs)}
    if prompt:
        for name, stack in zip(("k", "v"), kv_stacks):
            new_state[name] = jnp.transpose(stack.reshape(N_ODD, B, C_HEADS, C_HEAD_DIM, L), (0, 1, 4, 2, 3))
    else:
        new_state.update({"k": jnp.stack(ks), "v": jnp.stack(vs), "gmlp_v": jnp.stack(gvs)})
    return h, new_state


def kernel(x_prompt, x_sample, p_prompt, p_sample, cache_conv_a, state_ssm_re, state_ssm_im, cache_k, cache_v, cache_logf, cache_ffn_conv, g_mix_pre, g_mix_post, g_ffn_pre, g_ffn_post, w_even_in, w_conv_a, s5_a_re, s5_a_im, s5_log_dt, s5_b_re, s5_b_im, s5_c_re, s5_c_im, s5_d, w_glu, w_even_out, w_odd_in, b_forget, w_spatial, b_spatial, g_gmlp_v, w_odd_out, w_ffn_up, w_ffn_conv, w_ffn_down, w_ple, w_ple_gate):
    W = {"g_mix_pre": g_mix_pre, "g_mix_post": g_mix_post, "g_ffn_pre": g_ffn_pre, "g_ffn_post": g_ffn_post,
         "w_even_in": w_even_in, "w_conv_a": w_conv_a, "s5_a_re": s5_a_re, "s5_a_im": s5_a_im,
         "s5_log_dt": s5_log_dt, "s5_b_re": s5_b_re, "s5_b_im": s5_b_im, "s5_c_re": s5_c_re,
         "s5_c_im": s5_c_im, "s5_d": s5_d, "w_glu": w_glu, "w_even_out": w_even_out,
         "w_odd_in": w_odd_in, "b_forget": b_forget, "w_spatial": w_spatial, "b_spatial": b_spatial,
         "g_gmlp_v": g_gmlp_v, "w_odd_out": w_odd_out, "w_ffn_up": w_ffn_up, "w_ffn_conv": w_ffn_conv,
         "w_ffn_down": w_ffn_down, "w_ple": w_ple, "w_ple_gate": w_ple_gate}
    LW = [_layer_params(i, W) for i in range(DEPTH)]
    y_prompt, sp = _run_trunk(x_prompt, p_prompt, None, LW, bB=1, tT=256, tT_s5=32, tq=256, tT_ffn=512)
    dec_b, dec_l, _ = x_sample.shape
    past = cache_k.shape[2]
    st = {"conv_a": cache_conv_a, "ssm_re": state_ssm_re, "ssm_im": state_ssm_im, "ffn": cache_ffn_conv,
          "k_t": jnp.transpose(cache_k, (0, 1, 3, 4, 2)).reshape(N_ODD, dec_b, C_WIDTH, past),
          "v_t": jnp.transpose(cache_v, (0, 1, 3, 4, 2)).reshape(N_ODD, dec_b, C_WIDTH, past),
          "logf_t": jnp.swapaxes(cache_logf, 2, 3)}
    y_sample, ss = _run_trunk(x_sample, p_sample, st, LW, bB=dec_b, tT=dec_l, tT_s5=dec_l, tq=dec_l, tT_ffn=dec_l)
    return (y_prompt, y_sample,
            sp["conv_a"], sp["ssm_re"], sp["ssm_im"], sp["k"], sp["v"], sp["logf"], sp["ffn"],
            ss["conv_a"], ss["ssm_re"], ss["ssm_im"], ss["k"], ss["v"], ss["logf"], ss["gmlp_v"], ss["ffn"])
```

```python
import functools

import jax
import jax.numpy as jnp
import numpy as np
from jax import lax
from jax.experimental import pallas as pl
from jax.experimental.pallas import tpu as pltpu

F32 = jnp.float32
BF16 = jnp.bfloat16

D_MODEL = 1024
DEPTH = 4
N_ODD = DEPTH // 2
PLE_DIM = 256
RMS_EPS = 1e-6
A_WIDTH = 512
A_CONV = 3
B_WIDTH = 512
S5_GROUP = 16
S5_GROUPS = 32
S5_STATE = 64
S5_PAIRS = S5_GROUPS // 2
S5_QUADS = B_WIDTH // 128
C_HEADS = 8
C_HEAD_DIM = 64
C_WIDTH = 512
HEAD_PAIRS = C_HEADS // 2
D_WIDTH = 512
GMLP_CHUNK = 128
D_FF = 2816
FFN_CONV = 3
FF_CHUNK = 256
N_FF_CHUNKS = D_FF // FF_CHUNK
MASK_VALUE = -1e30
CONV_PAD = 8
HIST = A_CONV - 1

V7X_VMEM_LIMIT = 56 * 1024 * 1024
LANES = 128
LOG2E = 1.4426950408889634


def _rms(x, g):
    return x * lax.rsqrt(jnp.mean(x * x, axis=-1, keepdims=True) + RMS_EPS) * g


GELU_A = 0.7978845608028654
GELU_B = GELU_A * 0.044715


def _gelu_tanh(x):
    half = 0.5 * x
    return half + half * jnp.tanh(x * (GELU_A + GELU_B * (x * x)))


def _dot(a, b):
    return jnp.dot(a, b, preferred_element_type=F32)


def _dot_nt(a, b):
    return lax.dot_general(a, b, (((1,), (1,)), ((), ())), preferred_element_type=F32)


def _dot_exact(a, b):
    return jnp.dot(a, b, preferred_element_type=F32, precision=lax.Precision.HIGHEST)


def _const_spec(shape):
    zeros = (0,) * len(shape)
    return pl.BlockSpec(shape, lambda *_: zeros, pipeline_mode=pl.Buffered(1))


def _params(*sem):
    return pltpu.CompilerParams(dimension_semantics=sem, vmem_limit_bytes=V7X_VMEM_LIMIT)


def _causal_conv3(cb_ref, cur, hist, w, tT):
    cb_ref[:, CONV_PAD - HIST:CONV_PAD, :] = hist
    cb_ref[:, CONV_PAD:CONV_PAD + tT, :] = cur
    y = w[0:1][None] * cb_ref[:, CONV_PAD - 2:CONV_PAD - 2 + tT, :]
    y = y + w[1:2][None] * cb_ref[:, CONV_PAD - 1:CONV_PAD - 1 + tT, :]
    y = y + w[2:3][None] * cur
    return y, cb_ref[:, CONV_PAD + tT - HIST:CONV_PAD + tT, :]


def _even_in_kernel(h_ref, g_ref, w_ref, wc_ref, hist_ref, ya_ref, u_ref, nh_ref, cb_ref, *, bB, tT):
    @pl.when(pl.program_id(1) == 0)
    def _():
        nh_ref[...] = hist_ref[...]

    x = h_ref[...].reshape(bB * tT, D_MODEL)
    hn = _rms(x, g_ref[...])
    z = _dot(hn.astype(BF16), w_ref[...])
    gb = z[:, 0:A_WIDTH]
    gc = z[:, A_WIDTH:2 * A_WIDTH]
    xa = z[:, 2 * A_WIDTH:3 * A_WIDTH]
    u = z[:, 3 * A_WIDTH:]
    cx = (gc * xa).reshape(bB, tT, A_WIDTH)
    yc, last = _causal_conv3(cb_ref, cx, nh_ref[...], wc_ref[...], tT)
    nh_ref[...] = last
    ya_ref[...] = (gb.reshape(bB, tT, A_WIDTH) * yc).astype(BF16)
    for b in range(bB):
        u_ref[:, b * B_WIDTH:(b + 1) * B_WIDTH] = u[b * tT:(b + 1) * tT]


def _even_in(h, g, w, wc, hist, bB, tT):
    B, L, _ = h.shape
    grid = (B // bB, L // tT)
    kern = functools.partial(_even_in_kernel, bB=bB, tT=tT)
    return pl.pallas_call(
        kern,
        grid=grid,
        in_specs=[
            pl.BlockSpec((bB, tT, D_MODEL), lambda b, t: (b, t, 0)),
            _const_spec((1, D_MODEL)),
            _const_spec((D_MODEL, 4 * A_WIDTH)),
            _const_spec((A_CONV, A_WIDTH)),
            pl.BlockSpec((bB, HIST, A_WIDTH), lambda b, t: (b, 0, 0)),
        ],
        out_specs=[
            pl.BlockSpec((bB, tT, A_WIDTH), lambda b, t: (b, t, 0)),
            pl.BlockSpec((tT, bB * B_WIDTH), lambda b, t: (t, b)),
            pl.BlockSpec((bB, HIST, A_WIDTH), lambda b, t: (b, 0, 0)),
        ],
        out_shape=[
            jax.ShapeDtypeStruct((B, L, A_WIDTH), BF16),
            jax.ShapeDtypeStruct((L, B * B_WIDTH), F32),
            jax.ShapeDtypeStruct((B, HIST, A_WIDTH), F32),
        ],
        scratch_shapes=[pltpu.VMEM((bB, tT + CONV_PAD, A_WIDTH), F32)],
        compiler_params=_params("arbitrary", "arbitrary"),
        name="even_in",
    )(h, g, w, wc, hist)


S5_SCAN_PAIRS = 4


def _s5_kernel(u_ref, are_ref, aim_ref, ldt_ref, bre_ref, bim_ref, cre_ref, cim_ref, d_ref, wglu_ref,
               h0r_ref, h0i_ref, yb_ref, hr_ref, hi_ref, lam_ref, wb_ref, wc_ref, sre_ref, sim_ref, *, B, tT):
    M = tT * B

    @pl.when(pl.program_id(0) == 0)
    def _():
        a_re = are_ref[...]
        a_im = aim_ref[...]
        dt = jnp.exp(ldt_ref[...])
        mag = jnp.exp(a_re * dt)
        ab_re = mag * jnp.cos(a_im * dt)
        ab_im = mag * jnp.sin(a_im * dt)
        z_re = ab_re - 1.0
        den = a_re * a_re + a_im * a_im
        f_re = (z_re * a_re + ab_im * a_im) / den
        f_im = (ab_im * a_re - z_re * a_im) / den
        lam_ref[0] = ab_re
        lam_ref[1] = ab_im
        for j in range(S5_PAIRS):
            fr = f_re[j:j + 1]
            fi = f_im[j:j + 1]
            br = bre_ref[j]
            bi = bim_ref[j]
            wb_ref[j, :, 0:LANES] = (fr * br - fi * bi).astype(BF16)
            wb_ref[j, :, LANES:2 * LANES] = (fr * bi + fi * br).astype(BF16)
            wc_ref[j, 0:LANES, :] = cre_ref[j].astype(BF16)
            wc_ref[j, LANES:2 * LANES, :] = (-cim_ref[j]).astype(BF16)
        hr_ref[...] = h0r_ref[...]
        hi_ref[...] = h0i_ref[...]

    u = u_ref[...].reshape(M, B_WIDTH)
    ub = u.astype(BF16)
    for j in range(S5_PAIRS):
        q = j // (S5_PAIRS // S5_QUADS)
        bu = _dot(ub[:, q * LANES:(q + 1) * LANES], wb_ref[j])
        sre_ref[j] = bu[:, 0:LANES]
        sim_ref[j] = bu[:, LANES:2 * LANES]

    for j0 in range(0, S5_PAIRS, S5_SCAN_PAIRS):
        pairs = range(j0, j0 + S5_SCAN_PAIRS)
        lr = [lam_ref[0, j:j + 1, :] for j in pairs]
        li = [lam_ref[1, j:j + 1, :] for j in pairs]
        hr0 = tuple(hr_ref[:, j * LANES:(j + 1) * LANES] for j in pairs)
        hi0 = tuple(hi_ref[:, j * LANES:(j + 1) * LANES] for j in pairs)

        def step(t, carry, lr=lr, li=li, pairs=pairs):
            hr, hi = carry
            row = pl.multiple_of(t * B, B)
            nr, ni = [], []
            for k, j in enumerate(pairs):
                br = sre_ref[j, pl.ds(row, B), :]
                bi = sim_ref[j, pl.ds(row, B), :]
                r = lr[k] * hr[k] - li[k] * hi[k] + br
                i = lr[k] * hi[k] + li[k] * hr[k] + bi
                sre_ref[j, pl.ds(row, B), :] = r
                sim_ref[j, pl.ds(row, B), :] = i
                nr.append(r)
                ni.append(i)
            return tuple(nr), tuple(ni)

        hr, hi = lax.fori_loop(0, tT, step, (hr0, hi0))
        for k, j in enumerate(pairs):
            hr_ref[:, j * LANES:(j + 1) * LANES] = hr[k]
            hi_ref[:, j * LANES:(j + 1) * LANES] = hi[k]

    d = d_ref[...]
    cols = []
    per_quad = S5_PAIRS // S5_QUADS
    for q in range(S5_QUADS):
        acc = None
        for j in range(q * per_quad, (q + 1) * per_quad):
            hs = jnp.concatenate([sre_ref[j], sim_ref[j]], axis=-1).astype(BF16)
            part = _dot(hs, wc_ref[j])
            acc = part if acc is None else acc + part
        cols.append(acc + d[:, q * LANES:(q + 1) * LANES] * u[:, q * LANES:(q + 1) * LANES])
    y = jnp.concatenate(cols, axis=-1)
    g = _gelu_tanh(y)
    out = g * jax.nn.sigmoid(_dot(g.astype(BF16), wglu_ref[...]))
    yb_ref[...] = out.reshape(tT, B, B_WIDTH)


def _s5(u_tm, prm, h0r, h0i, B, L, tT):
    M = tT * B
    state = S5_PAIRS * LANES
    kern = functools.partial(_s5_kernel, B=B, tT=tT)
    tile = lambda i: (i, 0, 0)
    return pl.pallas_call(
        kern,
        grid=(L // tT,),
        in_specs=[
            pl.BlockSpec((tT, B, B_WIDTH), tile),
            _const_spec((S5_PAIRS, LANES)),
            _const_spec((S5_PAIRS, LANES)),
            _const_spec((S5_PAIRS, LANES)),
            _const_spec((S5_PAIRS, LANES, LANES)),
            _const_spec((S5_PAIRS, LANES, LANES)),
            _const_spec((S5_PAIRS, LANES, LANES)),
            _const_spec((S5_PAIRS, LANES, LANES)),
            _const_spec((1, B_WIDTH)),
            _const_spec((B_WIDTH, B_WIDTH)),
            _const_spec((B, state)),
            _const_spec((B, state)),
        ],
        out_specs=[
            pl.BlockSpec((tT, B, B_WIDTH), tile),
            pl.BlockSpec((B, state), lambda i: (0, 0)),
            pl.BlockSpec((B, state), lambda i: (0, 0)),
        ],
        out_shape=[
            jax.ShapeDtypeStruct((L, B, B_WIDTH), F32),
            jax.ShapeDtypeStruct((B, state), F32),
            jax.ShapeDtypeStruct((B, state), F32),
        ],
        scratch_shapes=[
            pltpu.VMEM((2, S5_PAIRS, LANES), F32),
            pltpu.VMEM((S5_PAIRS, LANES, 2 * LANES), BF16),
            pltpu.VMEM((S5_PAIRS, 2 * LANES, LANES), BF16),
            pltpu.VMEM((S5_PAIRS, M, LANES), F32),
            pltpu.VMEM((S5_PAIRS, M, LANES), F32),
        ],
        compiler_params=_params("arbitrary"),
        name="s5_scan",
    )(u_tm.reshape(L, B, B_WIDTH), prm["a_re"], prm["a_im"], prm["ldt"], prm["b_re"], prm["b_im"],
      prm["c_re"], prm["c_im"], prm["d"], prm["w_glu"], h0r, h0i)


def _odd_in_kernel(*refs, bB, tT, lc, kv_channel_major, n_alias):
    (h_ref, g_ref, w_ref, wkv_ref, wft_ref, bft_ref, gv_ref, ws_ref, bs_ref) = refs[:9]
    outs = refs[9 + n_alias:]
    if kv_channel_major:
        q_ref, k_ref, v_ref, kb_ref, vb_ref, lf_ref, ft_ref, yd_ref, carry_ref = outs
        vn_ref = None
    else:
        q_ref, k_ref, v_ref, kb_ref, vb_ref, lf_ref, ft_ref, yd_ref, vn_ref, carry_ref = outs
    M = bB * tT

    @pl.when(pl.program_id(1) == 0)
    def _():
        carry_ref[...] = jnp.zeros_like(carry_ref)

    x = h_ref[...].reshape(M, D_MODEL)
    hn = _rms(x, g_ref[...]).astype(BF16)
    z = _dot(hn, w_ref[...])
    q = z[:, 0:C_WIDTH] * (C_HEAD_DIM ** -0.5 * LOG2E)
    u = z[:, C_WIDTH:C_WIDTH + D_WIDTH]
    vd = z[:, C_WIDTH + D_WIDTH:C_WIDTH + 2 * D_WIDTH]
    q_ref[...] = q.reshape(bB, tT, C_WIDTH).astype(BF16)
    if kv_channel_major:
        kv = _dot_nt(wkv_ref[...], hn)
        k_ref[0, 0] = kv[0:C_WIDTH]
        v_ref[0, 0] = kv[C_WIDTH:]
        kb_ref[0] = kv[0:C_WIDTH].astype(BF16)
        ones = jnp.ones((C_HEAD_DIM, M), BF16)
        for hh in range(C_HEADS):
            v_at = hh * LANES + (hh % 2) * C_HEAD_DIM
            ones_at = hh * LANES + (1 - hh % 2) * C_HEAD_DIM
            vb_ref[0, v_at:v_at + C_HEAD_DIM, :] = (
                kv[C_WIDTH + hh * C_HEAD_DIM:C_WIDTH + (hh + 1) * C_HEAD_DIM].astype(BF16))
            vb_ref[0, ones_at:ones_at + C_HEAD_DIM, :] = ones
    else:
        k = z[:, C_WIDTH + 2 * D_WIDTH:2 * C_WIDTH + 2 * D_WIDTH].reshape(bB, tT, C_WIDTH)
        v = z[:, 2 * C_WIDTH + 2 * D_WIDTH:].reshape(bB, tT, C_WIDTH)
        k_ref[...] = k
        v_ref[...] = v
        kb_ref[...] = k.astype(BF16)
        vb_ref[...] = v.astype(BF16)

    logf_t = jax.nn.log_sigmoid(_dot_nt(wft_ref[...], hn) + bft_ref[...])
    r = lax.broadcasted_iota(jnp.int32, (M, M), 0)
    c = lax.broadcasted_iota(jnp.int32, (M, M), 1)
    upper = jnp.where(((r // tT) == (c // tT)) & (r <= c), 1.0, 0.0).astype(F32)
    ft = _dot_exact(logf_t, upper) + carry_ref[...]
    for b in range(bB):
        lf_ref[b] = logf_t[:, b * tT:(b + 1) * tT]
        ft_ref[b] = ft[:, b * tT:(b + 1) * tT]
    if bB == 1:
        carry_ref[...] = ft[:, M - 1:M]

    vn = _rms(vd, gv_ref[...])
    if vn_ref is not None:
        vn_ref[...] = vn.reshape(bB, tT, D_WIDTH)
    vnb = vn.astype(BF16)
    rr = lax.broadcasted_iota(jnp.int32, (lc, lc), 0)
    cc = lax.broadcasted_iota(jnp.int32, (lc, lc), 1)
    wsm = [jnp.where(cc <= rr, ws_ref[hh], 0.0).astype(BF16) for hh in range(C_HEADS)]
    first_head = lax.broadcasted_iota(jnp.int32, (lc, LANES), 1) < C_HEAD_DIM
    bias = bs_ref[...]
    for ch in range(M // lc):
        rows = slice(ch * lc, (ch + 1) * lc)
        cols = []
        for hp in range(HEAD_PAIRS):
            vp = vnb[rows, hp * LANES:(hp + 1) * LANES]
            cols.append(jnp.where(first_head, _dot(wsm[2 * hp], vp), _dot(wsm[2 * hp + 1], vp)))
        mixed = jnp.concatenate(cols, axis=-1) + bias
        b, off = divmod(ch * lc, tT)
        yd_ref[b, off:off + lc, :] = (u[rows] * mixed).astype(BF16)


def _odd_in(h, lw, bB, tT, j, kv_stacks):
    B, L, _ = h.shape
    nT = L // tT
    assert bB == 1 or nT == 1
    kv_channel_major = kv_stacks is not None
    lc = lw["w_s"].shape[-1]
    tile = lambda b, t: (b, t, 0)
    chan = lambda b, t: (b, 0, t)

    def act(width, dtype):
        return pl.BlockSpec((bB, tT, width), tile), jax.ShapeDtypeStruct((B, L, width), dtype)

    heads = (pl.BlockSpec((bB, C_HEADS, tT), chan), jax.ShapeDtypeStruct((B, C_HEADS, L), F32))
    if kv_channel_major:
        assert bB == 1
        stack = (pl.BlockSpec((1, 1, C_WIDTH, tT), lambda b, t: (j, b, 0, t)),
                 jax.ShapeDtypeStruct((N_ODD, B, C_WIDTH, L), F32))
        chan_bf = (pl.BlockSpec((1, C_WIDTH, tT), chan), jax.ShapeDtypeStruct((B, C_WIDTH, L), BF16))
        chan_v = (pl.BlockSpec((1, C_HEADS * LANES, tT), chan), jax.ShapeDtypeStruct((B, C_HEADS * LANES, L), BF16))
        outs = [act(C_WIDTH, BF16), stack, stack, chan_bf, chan_v, heads, heads, act(D_WIDTH, BF16)]
        w_kv_spec = _const_spec((2 * C_WIDTH, D_MODEL))
    else:
        outs = [act(C_WIDTH, BF16), act(C_WIDTH, F32), act(C_WIDTH, F32), act(C_WIDTH, BF16), act(C_WIDTH, BF16),
                heads, heads, act(D_WIDTH, BF16), act(D_WIDTH, F32)]
        w_kv_spec = _const_spec((1, 1))
    alias_in = list(kv_stacks) if kv_channel_major else []
    n_fixed = 9
    kern = functools.partial(_odd_in_kernel, bB=bB, tT=tT, lc=lc, kv_channel_major=kv_channel_major,
                             n_alias=len(alias_in))
    w_in = lw["w_in_cm"] if kv_channel_major else lw["w_in_rm"]
    w_kv = lw["w_kv_t"] if kv_channel_major else jnp.zeros((1, 1), BF16)
    return pl.pallas_call(
        kern,
        grid=(B // bB, nT),
        in_specs=[
            pl.BlockSpec((bB, tT, D_MODEL), tile),
            _const_spec((1, D_MODEL)),
            _const_spec(w_in.shape),
            w_kv_spec,
            _const_spec((C_HEADS, D_MODEL)),
            _const_spec((C_HEADS, 1)),
            _const_spec((1, D_WIDTH)),
            _const_spec((C_HEADS, lc, lc)),
            _const_spec((lc, D_WIDTH)),
        ] + [pl.BlockSpec(memory_space=pl.ANY)] * len(alias_in),
        out_specs=[o[0] for o in outs],
        out_shape=[o[1] for o in outs],
        input_output_aliases={n_fixed + i: 1 + i for i in range(len(alias_in))},
        scratch_shapes=[pltpu.VMEM((C_HEADS, 1), F32)],
        compiler_params=_params("arbitrary", "arbitrary"),
        name="odd_in",
    )(h, lw["g_pre"], w_in, w_kv, lw["w_ft"], lw["b_ft"], lw["g_v"], lw["w_s"], lw["b_s"], *alias_in)


def _attn_prompt_kernel(qi_ref, kj_ref, q_ref, kt_ref, vt_ref, ft_ref, o_ref, m_ref, acc_ref, *, tq):
    p = pl.program_id(1)
    qi = qi_ref[p]
    kj = kj_ref[p]
    first_head = lax.broadcasted_iota(jnp.int32, (tq, LANES), 1) < C_HEAD_DIM

    @pl.when(kj == 0)
    def _():
        m_ref[...] = jnp.full_like(m_ref, -jnp.inf)
        acc_ref[...] = jnp.zeros_like(acc_ref)

    def block(diagonal):
        if diagonal:
            visible = (lax.broadcasted_iota(jnp.int32, (tq, tq), 1)
                       <= lax.broadcasted_iota(jnp.int32, (tq, tq), 0))
        fk = ft_ref[0] * LOG2E
        for hp in range(HEAD_PAIRS):
            lanes = slice(hp * LANES, (hp + 1) * LANES)
            qp = q_ref[0, :, lanes]
            kt = kt_ref[0, lanes, :]
            qs = (jnp.where(first_head, qp, jnp.zeros_like(qp)), jnp.where(first_head, jnp.zeros_like(qp), qp))
            done = []
            for e in range(2):
                h = 2 * hp + e
                s = _dot(qs[e], kt) - fk[h:h + 1]
                if diagonal:
                    s = jnp.where(visible, s, MASK_VALUE)
                m_old = m_ref[h]
                m_new = jnp.maximum(m_old, jnp.max(s, axis=-1, keepdims=True))
                a = jnp.exp2(m_old - m_new)
                pr = jnp.exp2(s - jnp.concatenate([m_new] * (tq // LANES), axis=-1))
                acc = acc_ref[h] * a + _dot_nt(pr.astype(BF16), vt_ref[0, h * LANES:(h + 1) * LANES, :])
                if diagonal:
                    done.append(acc)
                else:
                    m_ref[h] = m_new
                    acc_ref[h] = acc
            if diagonal:
                num = jnp.where(first_head, done[0], done[1])
                den = pltpu.roll(jnp.where(first_head, done[1], done[0]), C_HEAD_DIM, axis=1)
                o_ref[0, :, lanes] = (num / den).astype(BF16)

    @pl.when(kj < qi)
    def _():
        block(False)

    @pl.when(kj == qi)
    def _():
        block(True)


def _attn_prompt(q, ktb, vtb, ft, tq):
    B, L, _ = q.shape
    nq = L // tq
    pairs = [(i, k) for i in range(nq) for k in range(i + 1)]
    qi = jnp.asarray(np.array([p[0] for p in pairs], np.int32))
    kj = jnp.asarray(np.array([p[1] for p in pairs], np.int32))
    kern = functools.partial(_attn_prompt_kernel, tq=tq)
    q_spec = pl.BlockSpec((1, tq, C_WIDTH), lambda b, p, qi, kj: (b, qi[p], 0))
    return pl.pallas_call(
        kern,
        grid_spec=pltpu.PrefetchScalarGridSpec(
            num_scalar_prefetch=2,
            grid=(B, len(pairs)),
            in_specs=[q_spec,
                      pl.BlockSpec((1, C_WIDTH, tq), lambda b, p, qi, kj: (b, 0, kj[p])),
                      pl.BlockSpec((1, C_HEADS * LANES, tq), lambda b, p, qi, kj: (b, 0, kj[p])),
                      pl.BlockSpec((1, C_HEADS, tq), lambda b, p, qi, kj: (b, 0, kj[p]))],
            out_specs=q_spec,
            scratch_shapes=[pltpu.VMEM((C_HEADS, tq, LANES), F32), pltpu.VMEM((C_HEADS, tq, LANES), F32)],
        ),
        out_shape=jax.ShapeDtypeStruct((B, L, C_WIDTH), BF16),
        compiler_params=_params("arbitrary", "arbitrary"),
        name="attn_prompt",
    )(qi, kj, q, ktb, vtb, ft)


def _attn_sample_kernel(q_ref, kn_ref, vn_ref, kpt_ref, vpt_ref, lpt_ref, ft_ref, o_ref, *, L, P):
    first_head = lax.broadcasted_iota(jnp.int32, (L, LANES), 1) < C_HEAD_DIM
    r = lax.broadcasted_iota(jnp.int32, (P, P), 0)
    c = lax.broadcasted_iota(jnp.int32, (P, P), 1)
    upper = jnp.where(r <= c, 1.0, 0.0).astype(F32)
    past_cum = _dot_exact(lpt_ref[0, 0], upper)
    past_after = (past_cum[:, P - 1:P] - past_cum) * LOG2E
    causal = lax.broadcasted_iota(jnp.int32, (L, L), 1) <= lax.broadcasted_iota(jnp.int32, (L, L), 0)
    ft = ft_ref[0] * LOG2E
    for hp in range(HEAD_PAIRS):
        lanes = slice(hp * LANES, (hp + 1) * LANES)
        qp = q_ref[0, :, lanes]
        qs = (jnp.where(first_head, qp, jnp.zeros_like(qp)), jnp.where(first_head, jnp.zeros_like(qp), qp))
        kpt = kpt_ref[0, 0, lanes, :].astype(BF16)
        vpt = vpt_ref[0, 0, lanes, :].astype(BF16)
        kn = kn_ref[0, :, lanes]
        vn = vn_ref[0, :, lanes]
        outs = []
        for e in range(2):
            h = 2 * hp + e
            sp = _dot(qs[e], kpt) + past_after[h:h + 1]
            sn = jnp.where(causal, _dot_nt(qs[e], kn) - ft[h:h + 1], MASK_VALUE)
            m = jnp.maximum(jnp.max(sp, axis=-1, keepdims=True), jnp.max(sn, axis=-1, keepdims=True))
            pp = jnp.exp2(sp - m)
            pn = jnp.exp2(sn - m)
            den = jnp.sum(pp, axis=-1, keepdims=True) + jnp.sum(pn, axis=-1, keepdims=True)
            outs.append((_dot_nt(pp.astype(BF16), vpt) + _dot(pn.astype(BF16), vn)) / den)
        o_ref[0, :, lanes] = jnp.where(first_head, outs[0], outs[1]).astype(BF16)


def _attn_sample(q, kb, vb, k_past_t, v_past_t, logf_past_t, ft, j):
    B, L, _ = q.shape
    P = k_past_t.shape[-1]
    kern = functools.partial(_attn_sample_kernel, L=L, P=P)
    new = pl.BlockSpec((1, L, C_WIDTH), lambda b: (b, 0, 0))
    past = pl.BlockSpec((1, 1, C_WIDTH, P), lambda b: (j, b, 0, 0))
    return pl.pallas_call(
        kern,
        grid=(B,),
        in_specs=[new, new, new, past, past,
                  pl.BlockSpec((1, 1, C_HEADS, P), lambda b: (j, b, 0, 0)),
                  pl.BlockSpec((1, C_HEADS, L), lambda b: (b, 0, 0))],
        out_specs=new,
        out_shape=jax.ShapeDtypeStruct((B, L, C_WIDTH), BF16),
        compiler_params=_params("arbitrary"),
        name="attn_sample",
    )(q, kb, vb, k_past_t, v_past_t, logf_past_t, ft)


def _ffn_kernel(h_ref, a_ref, b_ref, wo_ref, gpost_ref, gpre_ref, wup_ref, wcv_ref, fh_ref, wdn_ref, gfp_ref,
                wgate_ref, pe_ref, wple_ref, o_ref, nf_ref, cb_ref, act_ref, *, bB, tT, b_time_major):
    M = bB * tT

    @pl.when(pl.program_id(1) == 0)
    def _():
        nf_ref[...] = fh_ref[0]

    a = a_ref[...].reshape(M, a_ref.shape[-1])
    if b_time_major:
        width = b_ref.shape[-1] // bB
        bpart = jnp.concatenate([b_ref[:, i * width:(i + 1) * width] for i in range(bB)], axis=0)
    else:
        bpart = b_ref[...].reshape(M, b_ref.shape[-1])
    ka = a.shape[-1]
    y = _dot(a.astype(BF16), wo_ref[0:ka, :]) + _dot(bpart.astype(BF16), wo_ref[ka:, :])
    h1 = h_ref[...].reshape(M, D_MODEL) + _rms(y, gpost_ref[...])
    hn = _rms(h1, gpre_ref[...]).astype(BF16)

    def up_proj(c):
        return [_dot(hn, wup_ref[:, col0:col0 + FF_CHUNK]).reshape(bB, tT, FF_CHUNK)
                for col0 in (c * FF_CHUNK, D_FF + c * FF_CHUNK)]

    ups = up_proj(0)
    for c in range(N_FF_CHUNKS):
        nxt = up_proj(c + 1) if c + 1 < N_FF_CHUNKS else None
        cb = cb_ref.at[c % 2]
        halves = []
        for half, col0 in enumerate((c * FF_CHUNK, D_FF + c * FF_CHUNK)):
            cols = slice(col0, col0 + FF_CHUNK)
            uc, last = _causal_conv3(cb.at[:, :, half * FF_CHUNK:(half + 1) * FF_CHUNK],
                                     ups[half], nf_ref[:, :, cols], wcv_ref[:, cols], tT)
            nf_ref[:, :, cols] = last
            halves.append(uc)
        act_ref[:, c * FF_CHUNK:(c + 1) * FF_CHUNK] = (
            (_gelu_tanh(halves[0]) * halves[1]).reshape(M, FF_CHUNK).astype(BF16))
        ups = nxt

    h2 = h1 + _rms(_dot(act_ref[...], wdn_ref[...]), gfp_ref[...])
    gate = jax.nn.sigmoid(_dot(h2.astype(BF16), wgate_ref[...]))
    pe = pe_ref[0].reshape(M, PLE_DIM).astype(BF16)
    h3 = h2 + gate * _dot(pe, wple_ref[...])
    o_ref[...] = h3.reshape(bB, tT, D_MODEL)


def _ffn(h, a, b, b_time_major, lw, fh, pe, layer, bB, tT):
    B, L, _ = h.shape
    grid = (B // bB, L // tT)
    kern = functools.partial(_ffn_kernel, bB=bB, tT=tT, b_time_major=b_time_major)
    tile = lambda bi, t: (bi, t, 0)
    wa = a.shape[-1]
    if b_time_major:
        wb = b.shape[-1] // B
        b_spec = pl.BlockSpec((tT, bB * wb), lambda bi, t: (t, bi))
    else:
        wb = b.shape[-1]
        b_spec = pl.BlockSpec((bB, tT, wb), tile)
    fslot = layer if fh.shape[0] > 1 else 0
    return pl.pallas_call(
        kern,
        grid=grid,
        in_specs=[
            pl.BlockSpec((bB, tT, D_MODEL), tile),
            pl.BlockSpec((bB, tT, wa), tile),
            b_spec,
            _const_spec((wa + wb, D_MODEL)),
            _const_spec((1, D_MODEL)),
            _const_spec((1, D_MODEL)),
            _const_spec((D_MODEL, 2 * D_FF)),
            _const_spec((FFN_CONV, 2 * D_FF)),
            pl.BlockSpec((1, bB, HIST, 2 * D_FF), lambda bi, t: (fslot, bi, 0, 0)),
            _const_spec((D_FF, D_MODEL)),
            _const_spec((1, D_MODEL)),
            _const_spec((D_MODEL, D_MODEL)),
            pl.BlockSpec((1, bB, tT, PLE_DIM), lambda bi, t: (layer, bi, t, 0)),
            _const_spec((PLE_DIM, D_MODEL)),
        ],
        out_specs=[pl.BlockSpec((bB, tT, D_MODEL), tile),
                   pl.BlockSpec((bB, HIST, 2 * D_FF), lambda bi, t: (bi, 0, 0))],
        out_shape=[
            jax.ShapeDtypeStruct((B, L, D_MODEL), F32),
            jax.ShapeDtypeStruct((B, HIST, 2 * D_FF), F32),
        ],
        scratch_shapes=[
            pltpu.VMEM((2, bB, tT + CONV_PAD, 2 * FF_CHUNK), F32),
            pltpu.VMEM((bB * tT, D_FF), BF16),
        ],
        compiler_params=_params("arbitrary", "arbitrary"),
        name="mix_out_ffn",
    )(h, a, b, lw["w_out"], lw["g_post"], lw["g_ffn_pre"], lw["w_up"], lw["w_fconv"], fh, lw["w_down"],
      lw["g_ffn_post"], lw["w_gate"], pe, lw["w_ple"])


def _s5_block_layout(x, state_major):
    per_quad = S5_PAIRS // S5_QUADS
    eye2 = jnp.eye(2, dtype=x.dtype)
    slot = jax.nn.one_hot(jnp.arange(S5_PAIRS) % per_quad, per_quad, dtype=x.dtype)
    if state_major:
        x4 = x.reshape(S5_PAIRS, 2, S5_STATE, S5_GROUP)
        blk = jnp.einsum("jgpc,gh->jgchp", x4, eye2).reshape(S5_PAIRS, 2 * S5_GROUP, LANES)
        return jnp.einsum("jkn,jr->jrkn", blk, slot).reshape(S5_PAIRS, LANES, LANES)
    x4 = x.reshape(S5_PAIRS, 2, S5_GROUP, S5_STATE)
    blk = jnp.einsum("jgcp,gh->jgphc", x4, eye2).reshape(S5_PAIRS, LANES, 2 * S5_GROUP)
    return jnp.einsum("jnk,jr->jnrk", blk, slot).reshape(S5_PAIRS, LANES, LANES)


def _layer_params(i, W):
    j = i // 2
    lw = {
        "g_pre": W["g_mix_pre"][i][None], "g_post": W["g_mix_post"][i][None],
        "g_ffn_pre": W["g_ffn_pre"][i][None], "g_ffn_post": W["g_ffn_post"][i][None],
        "w_up": W["w_ffn_up"][i].astype(BF16), "w_fconv": W["w_ffn_conv"][i],
        "w_down": W["w_ffn_down"][i].astype(BF16),
        "w_gate": W["w_ple_gate"][i].astype(BF16), "w_ple": W["w_ple"][i].astype(BF16),
    }
    if i % 2 == 0:
        lw.update({
            "w_in": W["w_even_in"][j].astype(BF16), "w_conv": W["w_conv_a"][j],
            "w_out": W["w_even_out"][j].astype(BF16),
            "s5": {
                "a_re": W["s5_a_re"][j].reshape(S5_PAIRS, LANES),
                "a_im": W["s5_a_im"][j].reshape(S5_PAIRS, LANES),
                "ldt": jnp.broadcast_to(W["s5_log_dt"][j][:, None], (S5_GROUPS, S5_STATE)).reshape(S5_PAIRS, LANES),
                "b_re": _s5_block_layout(W["s5_b_re"][j], True), "b_im": _s5_block_layout(W["s5_b_im"][j], True),
                "c_re": _s5_block_layout(W["s5_c_re"][j], False), "c_im": _s5_block_layout(W["s5_c_im"][j], False),
                "d": W["s5_d"][j].reshape(1, B_WIDTH), "w_glu": W["w_glu"][j].astype(BF16),
            },
        })
    else:
        w = W["w_odd_in"][j].astype(BF16)
        f0 = 3 * C_WIDTH
        w_q, w_kv, w_f, w_ud = w[:, :C_WIDTH], w[:, C_WIDTH:f0], w[:, f0:f0 + C_HEADS], w[:, f0 + C_HEADS:]
        lw.update({
            "w_in_cm": jnp.concatenate([w_q, w_ud], axis=1),
            "w_in_rm": jnp.concatenate([w_q, w_ud, w_kv], axis=1),
            "w_kv_t": w_kv.T, "w_ft": w_f.T, "b_ft": W["b_forget"][j][:, None],
            "g_v": W["g_gmlp_v"][j][None],
            "w_out": W["w_odd_out"][j].astype(BF16),
        })
        for lc in (GMLP_CHUNK, 32):
            lw[("w_s", lc)] = W["w_spatial"][j][:, :lc, :lc]
            lw[("b_s", lc)] = jnp.repeat(W["b_spatial"][j][:, :lc].T, D_WIDTH // C_HEADS, axis=1)
    return lw


def _run_trunk(x, pe, st, LW, bB, tT, tT_s5, tq, tT_ffn):
    B, L, _ = x.shape
    prompt = st is None
    lc = min(L, GMLP_CHUNK)
    h = x
    conv_a, ssm_re, ssm_im, ks, vs, lfs, gvs, ffs = [], [], [], [], [], [], [], []
    kv_stacks = ()
    fh = jnp.zeros((1, B, HIST, 2 * D_FF), F32) if prompt else st["ffn"]
    for i in range(DEPTH):
        j = i // 2
        lw = LW[i]
        if i % 2 == 0:
            if prompt:
                hist = jnp.zeros((B, HIST, A_WIDTH), F32)
                h0r = jnp.zeros((B, S5_GROUPS * S5_STATE), F32)
                h0i = h0r
            else:
                hist = st["conv_a"][j]
                h0r = st["ssm_re"][j].reshape(B, S5_GROUPS * S5_STATE)
                h0i = st["ssm_im"][j].reshape(B, S5_GROUPS * S5_STATE)
            ya, u_tm, new_hist = _even_in(h, lw["g_pre"], lw["w_in"], lw["w_conv"], hist, bB, tT)
            yb_tm, nr, ni = _s5(u_tm, lw["s5"], h0r, h0i, B, L, tT_s5)
            a, b, b_tm = ya, yb_tm.reshape(L, B * B_WIDTH), True
            conv_a.append(new_hist)
            ssm_re.append(nr.reshape(B, S5_GROUPS, S5_STATE))
            ssm_im.append(ni.reshape(B, S5_GROUPS, S5_STATE))
        else:
            lw = dict(lw, w_s=lw[("w_s", lc)], b_s=lw[("b_s", lc)])
            if prompt:
                q, k_stack, v_stack, ktb, vtb, logf_t, ft, yd = _odd_in(h, lw, bB, tT, j, kv_stacks)
                kv_stacks = (k_stack, v_stack)
                att = _attn_prompt(q, ktb, vtb, ft, tq)
            else:
                q, k, v, kb, vb, logf_t, ft, yd, vn = _odd_in(h, lw, bB, tT, j, None)
                att = _attn_sample(q, kb, vb, st["k_t"], st["v_t"], st["logf_t"], ft, j)
                ks.append(k.reshape(B, L, C_HEADS, C_HEAD_DIM))
                vs.append(v.reshape(B, L, C_HEADS, C_HEAD_DIM))
                gvs.append(vn)
            a, b, b_tm = att, yd, False
            lfs.append(logf_t)
        h, new_f = _ffn(h, a, b, b_tm, lw, fh, pe, i, bB, tT_ffn)
        ffs.append(new_f)
    new_state = {"conv_a": jnp.stack(conv_a), "ssm_re": jnp.stack(ssm_re), "ssm_im": jnp.stack(ssm_im),
                 "logf": jnp.swapaxes(jnp.stack(lfs), 2, 3), "ffn": jnp.stack(ffs)}
    if prompt:
        for name, stack in zip(("k", "v"), kv_stacks):
            new_state[name] = jnp.transpose(stack.reshape(N_ODD, B, C_HEADS, C_HEAD_DIM, L), (0, 1, 4, 2, 3))
    else:
        new_state.update({"k": jnp.stack(ks), "v": jnp.stack(vs), "gmlp_v": jnp.stack(gvs)})
    return h, new_state


def kernel(x_prompt, x_sample, p_prompt, p_sample, cache_conv_a, state_ssm_re, state_ssm_im, cache_k, cache_v, cache_logf, cache_ffn_conv, g_mix_pre, g_mix_post, g_ffn_pre, g_ffn_post, w_even_in, w_conv_a, s5_a_re, s5_a_im, s5_log_dt, s5_b_re, s5_b_im, s5_c_re, s5_c_im, s5_d, w_glu, w_even_out, w_odd_in, b_forget, w_spatial, b_spatial, g_gmlp_v, w_odd_out, w_ffn_up, w_ffn_conv, w_ffn_down, w_ple, w_ple_gate):
    W = {"g_mix_pre": g_mix_pre, "g_mix_post": g_mix_post, "g_ffn_pre": g_ffn_pre, "g_ffn_post": g_ffn_post,
         "w_even_in": w_even_in, "w_conv_a": w_conv_a, "s5_a_re": s5_a_re, "s5_a_im": s5_a_im,
         "s5_log_dt": s5_log_dt, "s5_b_re": s5_b_re, "s5_b_im": s5_b_im, "s5_c_re": s5_c_re,
         "s5_c_im": s5_c_im, "s5_d": s5_d, "w_glu": w_glu, "w_even_out": w_even_out,
         "w_odd_in": w_odd_in, "b_forget": b_forget, "w_spatial": w_spatial, "b_spatial": b_spatial,
         "g_gmlp_v": g_gmlp_v, "w_odd_out": w_odd_out, "w_ffn_up": w_ffn_up, "w_ffn_conv": w_ffn_conv,
         "w_ffn_down": w_ffn_down, "w_ple": w_ple, "w_ple_gate": w_ple_gate}
    LW = [_layer_params(i, W) for i in range(DEPTH)]
    y_prompt, sp = _run_trunk(x_prompt, p_prompt, None, LW, bB=1, tT=256, tT_s5=32, tq=256, tT_ffn=512)
    dec_b, dec_l, _ = x_sample.shape
    past = cache_k.shape[2]
    st = {"conv_a": cache_conv_a, "ssm_re": state_ssm_re, "ssm_im": state_ssm_im, "ffn": cache_ffn_conv,
          "k_t": jnp.transpose(cache_k, (0, 1, 3, 4, 2)).reshape(N_ODD, dec_b, C_WIDTH, past),
          "v_t": jnp.transpose(cache_v, (0, 1, 3, 4, 2)).reshape(N_ODD, dec_b, C_WIDTH, past),
          "logf_t": jnp.swapaxes(cache_logf, 2, 3)}
    y_sample, ss = _run_trunk(x_sample, p_sample, st, LW, bB=dec_b, tT=dec_l, tT_s5=dec_l, tq=dec_l, tT_ffn=dec_l)
    return (y_prompt, y_sample,
            sp["conv_a"], sp["ssm_re"], sp["ssm_im"], sp["k"], sp["v"], sp["logf"], sp["ffn"],
            ss["conv_a"], ss["ssm_re"], ss["ssm_im"], ss["k"], ss["v"], ss["logf"], ss["gmlp_v"], ss["ffn"])
```

```python
import functools

import jax
import jax.numpy as jnp
import numpy as np
from jax import lax
from jax.experimental import pallas as pl
from jax.experimental.pallas import tpu as pltpu

F32 = jnp.float32
BF16 = jnp.bfloat16

D_MODEL = 1024
DEPTH = 4
N_ODD = DEPTH // 2
PLE_DIM = 256
RMS_EPS = 1e-6
A_WIDTH = 512
A_CONV = 3
B_WIDTH = 512
S5_GROUP = 16
S5_GROUPS = 32
S5_STATE = 64
S5_PAIRS = S5_GROUPS // 2
S5_QUADS = B_WIDTH // 128
C_HEADS = 8
C_HEAD_DIM = 64
C_WIDTH = 512
HEAD_PAIRS = C_HEADS // 2
D_WIDTH = 512
GMLP_CHUNK = 128
D_FF = 2816
FFN_CONV = 3
FF_CHUNK = 256
N_FF_CHUNKS = D_FF // FF_CHUNK
MASK_VALUE = -1e30
CONV_PAD = 8
HIST = A_CONV - 1

V7X_VMEM_LIMIT = 56 * 1024 * 1024
LANES = 128
LOG2E = 1.4426950408889634


def _rms(x, g):
    return x * lax.rsqrt(jnp.mean(x * x, axis=-1, keepdims=True) + RMS_EPS) * g


GELU_A = 0.7978845608028654
GELU_B = GELU_A * 0.044715


def _gelu_tanh(x):
    half = 0.5 * x
    return half + half * jnp.tanh(x * (GELU_A + GELU_B * (x * x)))


def _dot(a, b):
    return jnp.dot(a, b, preferred_element_type=F32)


def _dot_nt(a, b):
    return lax.dot_general(a, b, (((1,), (1,)), ((), ())), preferred_element_type=F32)


def _dot_exact(a, b):
    return jnp.dot(a, b, preferred_element_type=F32, precision=lax.Precision.HIGHEST)


def _const_spec(shape):
    zeros = (0,) * len(shape)
    return pl.BlockSpec(shape, lambda *_: zeros, pipeline_mode=pl.Buffered(1))


def _params(*sem):
    return pltpu.CompilerParams(dimension_semantics=sem, vmem_limit_bytes=V7X_VMEM_LIMIT)


def _causal_conv3(cb_ref, cur, hist, w, tT):
    cb_ref[:, CONV_PAD - HIST:CONV_PAD, :] = hist
    cb_ref[:, CONV_PAD:CONV_PAD + tT, :] = cur
    y = w[0:1][None] * cb_ref[:, CONV_PAD - 2:CONV_PAD - 2 + tT, :]
    y = y + w[1:2][None] * cb_ref[:, CONV_PAD - 1:CONV_PAD - 1 + tT, :]
    y = y + w[2:3][None] * cur
    return y, cb_ref[:, CONV_PAD + tT - HIST:CONV_PAD + tT, :]


def _even_in_kernel(h_ref, g_ref, w_ref, wc_ref, hist_ref, ya_ref, u_ref, nh_ref, cb_ref, *, bB, tT):
    @pl.when(pl.program_id(1) == 0)
    def _():
        nh_ref[...] = hist_ref[...]

    x = h_ref[...].reshape(bB * tT, D_MODEL)
    hn = _rms(x, g_ref[...])
    z = _dot(hn.astype(BF16), w_ref[...])
    gb = z[:, 0:A_WIDTH]
    gc = z[:, A_WIDTH:2 * A_WIDTH]
    xa = z[:, 2 * A_WIDTH:3 * A_WIDTH]
    u = z[:, 3 * A_WIDTH:]
    cx = (gc * xa).reshape(bB, tT, A_WIDTH)
    yc, last = _causal_conv3(cb_ref, cx, nh_ref[...], wc_ref[...], tT)
    nh_ref[...] = last
    ya_ref[...] = (gb.reshape(bB, tT, A_WIDTH) * yc).astype(BF16)
    for b in range(bB):
        u_ref[:, b * B_WIDTH:(b + 1) * B_WIDTH] = u[b * tT:(b + 1) * tT]


def _even_in(h, g, w, wc, hist, bB, tT):
    B, L, _ = h.shape
    grid = (B // bB, L // tT)
    kern = functools.partial(_even_in_kernel, bB=bB, tT=tT)
    return pl.pallas_call(
        kern,
        grid=grid,
        in_specs=[
            pl.BlockSpec((bB, tT, D_MODEL), lambda b, t: (b, t, 0)),
            _const_spec((1, D_MODEL)),
            _const_spec((D_MODEL, 4 * A_WIDTH)),
            _const_spec((A_CONV, A_WIDTH)),
            pl.BlockSpec((bB, HIST, A_WIDTH), lambda b, t: (b, 0, 0)),
        ],
        out_specs=[
            pl.BlockSpec((bB, tT, A_WIDTH), lambda b, t: (b, t, 0)),
            pl.BlockSpec((tT, bB * B_WIDTH), lambda b, t: (t, b)),
            pl.BlockSpec((bB, HIST, A_WIDTH), lambda b, t: (b, 0, 0)),
        ],
        out_shape=[
            jax.ShapeDtypeStruct((B, L, A_WIDTH), BF16),
            jax.ShapeDtypeStruct((L, B * B_WIDTH), F32),
            jax.ShapeDtypeStruct((B, HIST, A_WIDTH), F32),
        ],
        scratch_shapes=[pltpu.VMEM((bB, tT + CONV_PAD, A_WIDTH), F32)],
        compiler_params=_params("arbitrary", "arbitrary"),
        name="even_in",
    )(h, g, w, wc, hist)


def _s5_kernel(u_ref, are_ref, aim_ref, ldt_ref, bre_ref, bim_ref, cre_ref, cim_ref, d_ref, wglu_ref,
               h0r_ref, h0i_ref, yb_ref, hr_ref, hi_ref, lam_ref, wb_ref, wc_ref, sre_ref, sim_ref, *, B, tT):
    M = tT * B

    @pl.when(pl.program_id(0) == 0)
    def _():
        a_re = are_ref[...]
        a_im = aim_ref[...]
        dt = jnp.exp(ldt_ref[...])
        mag = jnp.exp(a_re * dt)
        ab_re = mag * jnp.cos(a_im * dt)
        ab_im = mag * jnp.sin(a_im * dt)
        z_re = ab_re - 1.0
        den = a_re * a_re + a_im * a_im
        f_re = (z_re * a_re + ab_im * a_im) / den
        f_im = (ab_im * a_re - z_re * a_im) / den
        lam_ref[0] = ab_re
        lam_ref[1] = ab_im
        for j in range(S5_PAIRS):
            fr = f_re[j:j + 1]
            fi = f_im[j:j + 1]
            br = bre_ref[j]
            bi = bim_ref[j]
            wb_ref[j, :, 0:LANES] = (fr * br - fi * bi).astype(BF16)
            wb_ref[j, :, LANES:2 * LANES] = (fr * bi + fi * br).astype(BF16)
            wc_ref[j, 0:LANES, :] = cre_ref[j].astype(BF16)
            wc_ref[j, LANES:2 * LANES, :] = (-cim_ref[j]).astype(BF16)
        hr_ref[...] = h0r_ref[...]
        hi_ref[...] = h0i_ref[...]

    u = u_ref[...].reshape(M, B_WIDTH)
    ub = u.astype(BF16)
    d = d_ref[...]
    per_quad = S5_PAIRS // S5_QUADS
    cols = [None] * S5_QUADS

    def b_proj(q):
        for j in range(q * per_quad, (q + 1) * per_quad):
            bu = _dot(ub[:, q * LANES:(q + 1) * LANES], wb_ref[j])
            sre_ref[j] = bu[:, 0:LANES]
            sim_ref[j] = bu[:, LANES:2 * LANES]

    def scan(q):
        pairs = range(q * per_quad, (q + 1) * per_quad)
        lr = [lam_ref[0, j:j + 1, :] for j in pairs]
        li = [lam_ref[1, j:j + 1, :] for j in pairs]
        hr = [hr_ref[:, j * LANES:(j + 1) * LANES] for j in pairs]
        hi = [hi_ref[:, j * LANES:(j + 1) * LANES] for j in pairs]
        for t in range(tT):
            rows = slice(t * B, (t + 1) * B)
            for k, j in enumerate(pairs):
                r = lr[k] * hr[k] - li[k] * hi[k] + sre_ref[j, rows, :]
                i = lr[k] * hi[k] + li[k] * hr[k] + sim_ref[j, rows, :]
                sre_ref[j, rows, :] = r
                sim_ref[j, rows, :] = i
                hr[k], hi[k] = r, i
        for k, j in enumerate(pairs):
            hr_ref[:, j * LANES:(j + 1) * LANES] = hr[k]
            hi_ref[:, j * LANES:(j + 1) * LANES] = hi[k]

    def c_proj(q):
        acc = None
        for j in range(q * per_quad, (q + 1) * per_quad):
            hs = jnp.concatenate([sre_ref[j], sim_ref[j]], axis=-1).astype(BF16)
            part = _dot(hs, wc_ref[j])
            acc = part if acc is None else acc + part
        cols[q] = acc + d[:, q * LANES:(q + 1) * LANES] * u[:, q * LANES:(q + 1) * LANES]

    b_proj(0)
    for q in range(S5_QUADS):
        if q + 1 < S5_QUADS:
            b_proj(q + 1)
        scan(q)
        if q >= 1:
            c_proj(q - 1)
    c_proj(S5_QUADS - 1)
    y = jnp.concatenate(cols, axis=-1)
    g = _gelu_tanh(y)
    out = g * jax.nn.sigmoid(_dot(g.astype(BF16), wglu_ref[...]))
    yb_ref[...] = out.reshape(tT, B, B_WIDTH)


def _s5(u_tm, prm, h0r, h0i, B, L, tT):
    M = tT * B
    state = S5_PAIRS * LANES
    kern = functools.partial(_s5_kernel, B=B, tT=tT)
    tile = lambda i: (i, 0, 0)
    return pl.pallas_call(
        kern,
        grid=(L // tT,),
        in_specs=[
            pl.BlockSpec((tT, B, B_WIDTH), tile),
            _const_spec((S5_PAIRS, LANES)),
            _const_spec((S5_PAIRS, LANES)),
            _const_spec((S5_PAIRS, LANES)),
            _const_spec((S5_PAIRS, LANES, LANES)),
            _const_spec((S5_PAIRS, LANES, LANES)),
            _const_spec((S5_PAIRS, LANES, LANES)),
            _const_spec((S5_PAIRS, LANES, LANES)),
            _const_spec((1, B_WIDTH)),
            _const_spec((B_WIDTH, B_WIDTH)),
            _const_spec((B, state)),
            _const_spec((B, state)),
        ],
        out_specs=[
            pl.BlockSpec((tT, B, B_WIDTH), tile),
            pl.BlockSpec((B, state), lambda i: (0, 0)),
            pl.BlockSpec((B, state), lambda i: (0, 0)),
        ],
        out_shape=[
            jax.ShapeDtypeStruct((L, B, B_WIDTH), F32),
            jax.ShapeDtypeStruct((B, state), F32),
            jax.ShapeDtypeStruct((B, state), F32),
        ],
        scratch_shapes=[
            pltpu.VMEM((2, S5_PAIRS, LANES), F32),
            pltpu.VMEM((S5_PAIRS, LANES, 2 * LANES), BF16),
            pltpu.VMEM((S5_PAIRS, 2 * LANES, LANES), BF16),
            pltpu.VMEM((S5_PAIRS, M, LANES), F32),
            pltpu.VMEM((S5_PAIRS, M, LANES), F32),
        ],
        compiler_params=_params("arbitrary"),
        name="s5_scan",
    )(u_tm.reshape(L, B, B_WIDTH), prm["a_re"], prm["a_im"], prm["ldt"], prm["b_re"], prm["b_im"],
      prm["c_re"], prm["c_im"], prm["d"], prm["w_glu"], h0r, h0i)


def _odd_in_kernel(*refs, bB, tT, lc, kv_channel_major, n_alias):
    (h_ref, g_ref, w_ref, wkv_ref, wft_ref, bft_ref, gv_ref, ws_ref, bs_ref) = refs[:9]
    outs = refs[9 + n_alias:]
    if kv_channel_major:
        q_ref, k_ref, v_ref, kb_ref, vb_ref, lf_ref, ft_ref, yd_ref, carry_ref = outs
        vn_ref = None
    else:
        q_ref, k_ref, v_ref, kb_ref, vb_ref, lf_ref, ft_ref, yd_ref, vn_ref, carry_ref = outs
    M = bB * tT

    @pl.when(pl.program_id(1) == 0)
    def _():
        carry_ref[...] = jnp.zeros_like(carry_ref)

    x = h_ref[...].reshape(M, D_MODEL)
    hn = _rms(x, g_ref[...]).astype(BF16)
    z = _dot(hn, w_ref[...])
    q = z[:, 0:C_WIDTH] * (C_HEAD_DIM ** -0.5 * LOG2E)
    u = z[:, C_WIDTH:C_WIDTH + D_WIDTH]
    vd = z[:, C_WIDTH + D_WIDTH:C_WIDTH + 2 * D_WIDTH]
    q_ref[...] = q.reshape(bB, tT, C_WIDTH).astype(BF16)
    if kv_channel_major:
        kv = _dot_nt(wkv_ref[...], hn)
        k_ref[0, 0] = kv[0:C_WIDTH]
        v_ref[0, 0] = kv[C_WIDTH:]
        kb_ref[0] = kv[0:C_WIDTH].astype(BF16)
        ones = jnp.ones((C_HEAD_DIM, M), BF16)
        for hh in range(C_HEADS):
            v_at = hh * LANES + (hh % 2) * C_HEAD_DIM
            ones_at = hh * LANES + (1 - hh % 2) * C_HEAD_DIM
            vb_ref[0, v_at:v_at + C_HEAD_DIM, :] = (
                kv[C_WIDTH + hh * C_HEAD_DIM:C_WIDTH + (hh + 1) * C_HEAD_DIM].astype(BF16))
            vb_ref[0, ones_at:ones_at + C_HEAD_DIM, :] = ones
    else:
        k = z[:, C_WIDTH + 2 * D_WIDTH:2 * C_WIDTH + 2 * D_WIDTH].reshape(bB, tT, C_WIDTH)
        v = z[:, 2 * C_WIDTH + 2 * D_WIDTH:].reshape(bB, tT, C_WIDTH)
        k_ref[...] = k
        v_ref[...] = v
        kb_ref[...] = k.astype(BF16)
        vb_ref[...] = v.astype(BF16)

    logf_t = jax.nn.log_sigmoid(_dot_nt(wft_ref[...], hn) + bft_ref[...])
    r = lax.broadcasted_iota(jnp.int32, (M, M), 0)
    c = lax.broadcasted_iota(jnp.int32, (M, M), 1)
    upper = jnp.where(((r // tT) == (c // tT)) & (r <= c), 1.0, 0.0).astype(F32)
    ft = _dot_exact(logf_t, upper) + carry_ref[...]
    for b in range(bB):
        lf_ref[b] = logf_t[:, b * tT:(b + 1) * tT]
        ft_ref[b] = ft[:, b * tT:(b + 1) * tT]
    if bB == 1:
        carry_ref[...] = ft[:, M - 1:M]

    vn = _rms(vd, gv_ref[...])
    if vn_ref is not None:
        vn_ref[...] = vn.reshape(bB, tT, D_WIDTH)
    vnb = vn.astype(BF16)
    rr = lax.broadcasted_iota(jnp.int32, (lc, lc), 0)
    cc = lax.broadcasted_iota(jnp.int32, (lc, lc), 1)
    wsm = [jnp.where(cc <= rr, ws_ref[hh], 0.0).astype(BF16) for hh in range(C_HEADS)]
    first_head = lax.broadcasted_iota(jnp.int32, (lc, LANES), 1) < C_HEAD_DIM
    bias = bs_ref[...]
    for ch in range(M // lc):
        rows = slice(ch * lc, (ch + 1) * lc)
        cols = []
        for hp in range(HEAD_PAIRS):
            vp = vnb[rows, hp * LANES:(hp + 1) * LANES]
            cols.append(jnp.where(first_head, _dot(wsm[2 * hp], vp), _dot(wsm[2 * hp + 1], vp)))
        mixed = jnp.concatenate(cols, axis=-1) + bias
        b, off = divmod(ch * lc, tT)
        yd_ref[b, off:off + lc, :] = (u[rows] * mixed).astype(BF16)


def _odd_in(h, lw, bB, tT, j, kv_stacks):
    B, L, _ = h.shape
    nT = L // tT
    assert bB == 1 or nT == 1
    kv_channel_major = kv_stacks is not None
    lc = lw["w_s"].shape[-1]
    tile = lambda b, t: (b, t, 0)
    chan = lambda b, t: (b, 0, t)

    def act(width, dtype):
        return pl.BlockSpec((bB, tT, width), tile), jax.ShapeDtypeStruct((B, L, width), dtype)

    heads = (pl.BlockSpec((bB, C_HEADS, tT), chan), jax.ShapeDtypeStruct((B, C_HEADS, L), F32))
    if kv_channel_major:
        assert bB == 1
        stack = (pl.BlockSpec((1, 1, C_WIDTH, tT), lambda b, t: (j, b, 0, t)),
                 jax.ShapeDtypeStruct((N_ODD, B, C_WIDTH, L), F32))
        chan_bf = (pl.BlockSpec((1, C_WIDTH, tT), chan), jax.ShapeDtypeStruct((B, C_WIDTH, L), BF16))
        chan_v = (pl.BlockSpec((1, C_HEADS * LANES, tT), chan), jax.ShapeDtypeStruct((B, C_HEADS * LANES, L), BF16))
        outs = [act(C_WIDTH, BF16), stack, stack, chan_bf, chan_v, heads, heads, act(D_WIDTH, BF16)]
        w_kv_spec = _const_spec((2 * C_WIDTH, D_MODEL))
    else:
        outs = [act(C_WIDTH, BF16), act(C_WIDTH, F32), act(C_WIDTH, F32), act(C_WIDTH, BF16), act(C_WIDTH, BF16),
                heads, heads, act(D_WIDTH, BF16), act(D_WIDTH, F32)]
        w_kv_spec = _const_spec((1, 1))
    alias_in = list(kv_stacks) if kv_channel_major else []
    n_fixed = 9
    kern = functools.partial(_odd_in_kernel, bB=bB, tT=tT, lc=lc, kv_channel_major=kv_channel_major,
                             n_alias=len(alias_in))
    w_in = lw["w_in_cm"] if kv_channel_major else lw["w_in_rm"]
    w_kv = lw["w_kv_t"] if kv_channel_major else jnp.zeros((1, 1), BF16)
    return pl.pallas_call(
        kern,
        grid=(B // bB, nT),
        in_specs=[
            pl.BlockSpec((bB, tT, D_MODEL), tile),
            _const_spec((1, D_MODEL)),
            _const_spec(w_in.shape),
            w_kv_spec,
            _const_spec((C_HEADS, D_MODEL)),
            _const_spec((C_HEADS, 1)),
            _const_spec((1, D_WIDTH)),
            _const_spec((C_HEADS, lc, lc)),
            _const_spec((lc, D_WIDTH)),
        ] + [pl.BlockSpec(memory_space=pl.ANY)] * len(alias_in),
        out_specs=[o[0] for o in outs],
        out_shape=[o[1] for o in outs],
        input_output_aliases={n_fixed + i: 1 + i for i in range(len(alias_in))},
        scratch_shapes=[pltpu.VMEM((C_HEADS, 1), F32)],
        compiler_params=_params("arbitrary", "arbitrary"),
        name="odd_in",
    )(h, lw["g_pre"], w_in, w_kv, lw["w_ft"], lw["b_ft"], lw["g_v"], lw["w_s"], lw["b_s"], *alias_in)


def _attn_prompt_kernel(qi_ref, kj_ref, q_ref, kt_ref, vt_ref, ft_ref, o_ref, m_ref, acc_ref, *, tq):
    p = pl.program_id(1)
    qi = qi_ref[p]
    kj = kj_ref[p]
    first_head = lax.broadcasted_iota(jnp.int32, (tq, LANES), 1) < C_HEAD_DIM

    @pl.when(kj == 0)
    def _():
        m_ref[...] = jnp.full_like(m_ref, -jnp.inf)
        acc_ref[...] = jnp.zeros_like(acc_ref)

    def block(diagonal):
        if diagonal:
            visible = (lax.broadcasted_iota(jnp.int32, (tq, tq), 1)
                       <= lax.broadcasted_iota(jnp.int32, (tq, tq), 0))
        fk = ft_ref[0] * LOG2E
        for hp in range(HEAD_PAIRS):
            lanes = slice(hp * LANES, (hp + 1) * LANES)
            qp = q_ref[0, :, lanes]
            kt = kt_ref[0, lanes, :]
            qs = (jnp.where(first_head, qp, jnp.zeros_like(qp)), jnp.where(first_head, jnp.zeros_like(qp), qp))
            done = []
            for e in range(2):
                h = 2 * hp + e
                s = _dot(qs[e], kt) - fk[h:h + 1]
                if diagonal:
                    s = jnp.where(visible, s, MASK_VALUE)
                m_old = m_ref[h]
                m_new = jnp.maximum(m_old, jnp.max(s, axis=-1, keepdims=True))
                a = jnp.exp2(m_old - m_new)
                pr = jnp.exp2(s - jnp.concatenate([m_new] * (tq // LANES), axis=-1))
                acc = acc_ref[h] * a + _dot_nt(pr.astype(BF16), vt_ref[0, h * LANES:(h + 1) * LANES, :])
                if diagonal:
                    done.append(acc)
                else:
                    m_ref[h] = m_new
                    acc_ref[h] = acc
            if diagonal:
                num = jnp.where(first_head, done[0], done[1])
                den = pltpu.roll(jnp.where(first_head, done[1], done[0]), C_HEAD_DIM, axis=1)
                o_ref[0, :, lanes] = (num / den).astype(BF16)

    @pl.when(kj < qi)
    def _():
        block(False)

    @pl.when(kj == qi)
    def _():
        block(True)


def _attn_prompt(q, ktb, vtb, ft, tq):
    B, L, _ = q.shape
    nq = L // tq
    pairs = [(i, k) for i in range(nq) for k in range(i + 1)]
    qi = jnp.asarray(np.array([p[0] for p in pairs], np.int32))
    kj = jnp.asarray(np.array([p[1] for p in pairs], np.int32))
    kern = functools.partial(_attn_prompt_kernel, tq=tq)
    q_spec = pl.BlockSpec((1, tq, C_WIDTH), lambda b, p, qi, kj: (b, qi[p], 0))
    return pl.pallas_call(
        kern,
        grid_spec=pltpu.PrefetchScalarGridSpec(
            num_scalar_prefetch=2,
            grid=(B, len(pairs)),
            in_specs=[q_spec,
                      pl.BlockSpec((1, C_WIDTH, tq), lambda b, p, qi, kj: (b, 0, kj[p])),
                      pl.BlockSpec((1, C_HEADS * LANES, tq), lambda b, p, qi, kj: (b, 0, kj[p])),
                      pl.BlockSpec((1, C_HEADS, tq), lambda b, p, qi, kj: (b, 0, kj[p]))],
            out_specs=q_spec,
            scratch_shapes=[pltpu.VMEM((C_HEADS, tq, LANES), F32), pltpu.VMEM((C_HEADS, tq, LANES), F32)],
        ),
        out_shape=jax.ShapeDtypeStruct((B, L, C_WIDTH), BF16),
        compiler_params=_params("arbitrary", "arbitrary"),
        name="attn_prompt",
    )(qi, kj, q, ktb, vtb, ft)


def _attn_sample_kernel(q_ref, kn_ref, vn_ref, kpt_ref, vpt_ref, lpt_ref, ft_ref, o_ref, *, L, P):
    first_head = lax.broadcasted_iota(jnp.int32, (L, LANES), 1) < C_HEAD_DIM
    r = lax.broadcasted_iota(jnp.int32, (P, P), 0)
    c = lax.broadcasted_iota(jnp.int32, (P, P), 1)
    upper = jnp.where(r <= c, 1.0, 0.0).astype(F32)
    past_cum = _dot_exact(lpt_ref[0, 0], upper)
    past_after = (past_cum[:, P - 1:P] - past_cum) * LOG2E
    causal = lax.broadcasted_iota(jnp.int32, (L, L), 1) <= lax.broadcasted_iota(jnp.int32, (L, L), 0)
    ft = ft_ref[0] * LOG2E
    for hp in range(HEAD_PAIRS):
        lanes = slice(hp * LANES, (hp + 1) * LANES)
        qp = q_ref[0, :, lanes]
        qs = (jnp.where(first_head, qp, jnp.zeros_like(qp)), jnp.where(first_head, jnp.zeros_like(qp), qp))
        kpt = kpt_ref[0, 0, lanes, :].astype(BF16)
        vpt = vpt_ref[0, 0, lanes, :].astype(BF16)
        kn = kn_ref[0, :, lanes]
        vn = vn_ref[0, :, lanes]
        outs = []
        for e in range(2):
            h = 2 * hp + e
            sp = _dot(qs[e], kpt) + past_after[h:h + 1]
            sn = jnp.where(causal, _dot_nt(qs[e], kn) - ft[h:h + 1], MASK_VALUE)
            m = jnp.maximum(jnp.max(sp, axis=-1, keepdims=True), jnp.max(sn, axis=-1, keepdims=True))
            pp = jnp.exp2(sp - m)
            pn = jnp.exp2(sn - m)
            den = jnp.sum(pp, axis=-1, keepdims=True) + jnp.sum(pn, axis=-1, keepdims=True)
            outs.append((_dot_nt(pp.astype(BF16), vpt) + _dot(pn.astype(BF16), vn)) / den)
        o_ref[0, :, lanes] = jnp.where(first_head, outs[0], outs[1]).astype(BF16)


def _attn_sample(q, kb, vb, k_past_t, v_past_t, logf_past_t, ft, j):
    B, L, _ = q.shape
    P = k_past_t.shape[-1]
    kern = functools.partial(_attn_sample_kernel, L=L, P=P)
    new = pl.BlockSpec((1, L, C_WIDTH), lambda b: (b, 0, 0))
    past = pl.BlockSpec((1, 1, C_WIDTH, P), lambda b: (j, b, 0, 0))
    return pl.pallas_call(
        kern,
        grid=(B,),
        in_specs=[new, new, new, past, past,
                  pl.BlockSpec((1, 1, C_HEADS, P), lambda b: (j, b, 0, 0)),
                  pl.BlockSpec((1, C_HEADS, L), lambda b: (b, 0, 0))],
        out_specs=new,
        out_shape=jax.ShapeDtypeStruct((B, L, C_WIDTH), BF16),
        compiler_params=_params("arbitrary"),
        name="attn_sample",
    )(q, kb, vb, k_past_t, v_past_t, logf_past_t, ft)


def _ffn_kernel(h_ref, a_ref, b_ref, wo_ref, gpost_ref, gpre_ref, wup_ref, wcv_ref, fh_ref, wdn_ref, gfp_ref,
                wgate_ref, pe_ref, wple_ref, o_ref, nf_ref, cb_ref, act_ref, *, bB, tT, b_time_major):
    M = bB * tT

    @pl.when(pl.program_id(1) == 0)
    def _():
        nf_ref[...] = fh_ref[0]

    a = a_ref[...].reshape(M, a_ref.shape[-1])
    if b_time_major:
        width = b_ref.shape[-1] // bB
        bpart = jnp.concatenate([b_ref[:, i * width:(i + 1) * width] for i in range(bB)], axis=0)
    else:
        bpart = b_ref[...].reshape(M, b_ref.shape[-1])
    ka = a.shape[-1]
    y = _dot(a.astype(BF16), wo_ref[0:ka, :]) + _dot(bpart.astype(BF16), wo_ref[ka:, :])
    h1 = h_ref[...].reshape(M, D_MODEL) + _rms(y, gpost_ref[...])
    hn = _rms(h1, gpre_ref[...]).astype(BF16)

    def up_proj(c):
        return [_dot(hn, wup_ref[:, col0:col0 + FF_CHUNK]).reshape(bB, tT, FF_CHUNK)
                for col0 in (c * FF_CHUNK, D_FF + c * FF_CHUNK)]

    ups = up_proj(0)
    for c in range(N_FF_CHUNKS):
        nxt = up_proj(c + 1) if c + 1 < N_FF_CHUNKS else None
        cb = cb_ref.at[c % 2]
        halves = []
        for half, col0 in enumerate((c * FF_CHUNK, D_FF + c * FF_CHUNK)):
            cols = slice(col0, col0 + FF_CHUNK)
            uc, last = _causal_conv3(cb.at[:, :, half * FF_CHUNK:(half + 1) * FF_CHUNK],
                                     ups[half], nf_ref[:, :, cols], wcv_ref[:, cols], tT)
            nf_ref[:, :, cols] = last
            halves.append(uc)
        act_ref[:, c * FF_CHUNK:(c + 1) * FF_CHUNK] = (
            (_gelu_tanh(halves[0]) * halves[1]).reshape(M, FF_CHUNK).astype(BF16))
        ups = nxt

    h2 = h1 + _rms(_dot(act_ref[...], wdn_ref[...]), gfp_ref[...])
    gate = jax.nn.sigmoid(_dot(h2.astype(BF16), wgate_ref[...]))
    pe = pe_ref[0].reshape(M, PLE_DIM).astype(BF16)
    h3 = h2 + gate * _dot(pe, wple_ref[...])
    o_ref[...] = h3.reshape(bB, tT, D_MODEL)


def _ffn(h, a, b, b_time_major, lw, fh, pe, layer, bB, tT):
    B, L, _ = h.shape
    grid = (B // bB, L // tT)
    kern = functools.partial(_ffn_kernel, bB=bB, tT=tT, b_time_major=b_time_major)
    tile = lambda bi, t: (bi, t, 0)
    wa = a.shape[-1]
    if b_time_major:
        wb = b.shape[-1] // B
        b_spec = pl.BlockSpec((tT, bB * wb), lambda bi, t: (t, bi))
    else:
        wb = b.shape[-1]
        b_spec = pl.BlockSpec((bB, tT, wb), tile)
    fslot = layer if fh.shape[0] > 1 else 0
    return pl.pallas_call(
        kern,
        grid=grid,
        in_specs=[
            pl.BlockSpec((bB, tT, D_MODEL), tile),
            pl.BlockSpec((bB, tT, wa), tile),
            b_spec,
            _const_spec((wa + wb, D_MODEL)),
            _const_spec((1, D_MODEL)),
            _const_spec((1, D_MODEL)),
            _const_spec((D_MODEL, 2 * D_FF)),
            _const_spec((FFN_CONV, 2 * D_FF)),
            pl.BlockSpec((1, bB, HIST, 2 * D_FF), lambda bi, t: (fslot, bi, 0, 0)),
            _const_spec((D_FF, D_MODEL)),
            _const_spec((1, D_MODEL)),
            _const_spec((D_MODEL, D_MODEL)),
            pl.BlockSpec((1, bB, tT, PLE_DIM), lambda bi, t: (layer, bi, t, 0)),
            _const_spec((PLE_DIM, D_MODEL)),
        ],
        out_specs=[pl.BlockSpec((bB, tT, D_MODEL), tile),
                   pl.BlockSpec((bB, HIST, 2 * D_FF), lambda bi, t: (bi, 0, 0))],
        out_shape=[
            jax.ShapeDtypeStruct((B, L, D_MODEL), F32),
            jax.ShapeDtypeStruct((B, HIST, 2 * D_FF), F32),
        ],
        scratch_shapes=[
            pltpu.VMEM((2, bB, tT + CONV_PAD, 2 * FF_CHUNK), F32),
            pltpu.VMEM((bB * tT, D_FF), BF16),
        ],
        compiler_params=_params("arbitrary", "arbitrary"),
        name="mix_out_ffn",
    )(h, a, b, lw["w_out"], lw["g_post"], lw["g_ffn_pre"], lw["w_up"], lw["w_fconv"], fh, lw["w_down"],
      lw["g_ffn_post"], lw["w_gate"], pe, lw["w_ple"])


def _s5_block_layout(x, state_major):
    per_quad = S5_PAIRS // S5_QUADS
    eye2 = jnp.eye(2, dtype=x.dtype)
    slot = jax.nn.one_hot(jnp.arange(S5_PAIRS) % per_quad, per_quad, dtype=x.dtype)
    if state_major:
        x4 = x.reshape(S5_PAIRS, 2, S5_STATE, S5_GROUP)
        blk = jnp.einsum("jgpc,gh->jgchp", x4, eye2).reshape(S5_PAIRS, 2 * S5_GROUP, LANES)
        return jnp.einsum("jkn,jr->jrkn", blk, slot).reshape(S5_PAIRS, LANES, LANES)
    x4 = x.reshape(S5_PAIRS, 2, S5_GROUP, S5_STATE)
    blk = jnp.einsum("jgcp,gh->jgphc", x4, eye2).reshape(S5_PAIRS, LANES, 2 * S5_GROUP)
    return jnp.einsum("jnk,jr->jnrk", blk, slot).reshape(S5_PAIRS, LANES, LANES)


def _layer_params(i, W):
    j = i // 2
    lw = {
        "g_pre": W["g_mix_pre"][i][None], "g_post": W["g_mix_post"][i][None],
        "g_ffn_pre": W["g_ffn_pre"][i][None], "g_ffn_post": W["g_ffn_post"][i][None],
        "w_up": W["w_ffn_up"][i].astype(BF16), "w_fconv": W["w_ffn_conv"][i],
        "w_down": W["w_ffn_down"][i].astype(BF16),
        "w_gate": W["w_ple_gate"][i].astype(BF16), "w_ple": W["w_ple"][i].astype(BF16),
    }
    if i % 2 == 0:
        lw.update({
            "w_in": W["w_even_in"][j].astype(BF16), "w_conv": W["w_conv_a"][j],
            "w_out": W["w_even_out"][j].astype(BF16),
            "s5": {
                "a_re": W["s5_a_re"][j].reshape(S5_PAIRS, LANES),
                "a_im": W["s5_a_im"][j].reshape(S5_PAIRS, LANES),
                "ldt": jnp.broadcast_to(W["s5_log_dt"][j][:, None], (S5_GROUPS, S5_STATE)).reshape(S5_PAIRS, LANES),
                "b_re": _s5_block_layout(W["s5_b_re"][j], True), "b_im": _s5_block_layout(W["s5_b_im"][j], True),
                "c_re": _s5_block_layout(W["s5_c_re"][j], False), "c_im": _s5_block_layout(W["s5_c_im"][j], False),
                "d": W["s5_d"][j].reshape(1, B_WIDTH), "w_glu": W["w_glu"][j].astype(BF16),
            },
        })
    else:
        w = W["w_odd_in"][j].astype(BF16)
        f0 = 3 * C_WIDTH
        w_q, w_kv, w_f, w_ud = w[:, :C_WIDTH], w[:, C_WIDTH:f0], w[:, f0:f0 + C_HEADS], w[:, f0 + C_HEADS:]
        lw.update({
            "w_in_cm": jnp.concatenate([w_q, w_ud], axis=1),
            "w_in_rm": jnp.concatenate([w_q, w_ud, w_kv], axis=1),
            "w_kv_t": w_kv.T, "w_ft": w_f.T, "b_ft": W["b_forget"][j][:, None],
            "g_v": W["g_gmlp_v"][j][None],
            "w_out": W["w_odd_out"][j].astype(BF16),
        })
        for lc in (GMLP_CHUNK, 32):
            lw[("w_s", lc)] = W["w_spatial"][j][:, :lc, :lc]
            lw[("b_s", lc)] = jnp.repeat(W["b_spatial"][j][:, :lc].T, D_WIDTH // C_HEADS, axis=1)
    return lw


def _run_trunk(x, pe, st, LW, bB, tT, tT_s5, tq, tT_ffn):
    B, L, _ = x.shape
    prompt = st is None
    lc = min(L, GMLP_CHUNK)
    h = x
    conv_a, ssm_re, ssm_im, ks, vs, lfs, gvs, ffs = [], [], [], [], [], [], [], []
    kv_stacks = ()
    fh = jnp.zeros((1, B, HIST, 2 * D_FF), F32) if prompt else st["ffn"]
    for i in range(DEPTH):
        j = i // 2
        lw = LW[i]
        if i % 2 == 0:
            if prompt:
                hist = jnp.zeros((B, HIST, A_WIDTH), F32)
                h0r = jnp.zeros((B, S5_GROUPS * S5_STATE), F32)
                h0i = h0r
            else:
                hist = st["conv_a"][j]
                h0r = st["ssm_re"][j].reshape(B, S5_GROUPS * S5_STATE)
                h0i = st["ssm_im"][j].reshape(B, S5_GROUPS * S5_STATE)
            ya, u_tm, new_hist = _even_in(h, lw["g_pre"], lw["w_in"], lw["w_conv"], hist, bB, tT)
            yb_tm, nr, ni = _s5(u_tm, lw["s5"], h0r, h0i, B, L, tT_s5)
            a, b, b_tm = ya, yb_tm.reshape(L, B * B_WIDTH), True
            conv_a.append(new_hist)
            ssm_re.append(nr.reshape(B, S5_GROUPS, S5_STATE))
            ssm_im.append(ni.reshape(B, S5_GROUPS, S5_STATE))
        else:
            lw = dict(lw, w_s=lw[("w_s", lc)], b_s=lw[("b_s", lc)])
            if prompt:
                q, k_stack, v_stack, ktb, vtb, logf_t, ft, yd = _odd_in(h, lw, bB, tT, j, kv_stacks)
                kv_stacks = (k_stack, v_stack)
                att = _attn_prompt(q, ktb, vtb, ft, tq)
            else:
                q, k, v, kb, vb, logf_t, ft, yd, vn = _odd_in(h, lw, bB, tT, j, None)
                att = _attn_sample(q, kb, vb, st["k_t"], st["v_t"], st["logf_t"], ft, j)
                ks.append(k.reshape(B, L, C_HEADS, C_HEAD_DIM))
                vs.append(v.reshape(B, L, C_HEADS, C_HEAD_DIM))
                gvs.append(vn)
            a, b, b_tm = att, yd, False
            lfs.append(logf_t)
        h, new_f = _ffn(h, a, b, b_tm, lw, fh, pe, i, bB, tT_ffn)
        ffs.append(new_f)
    new_state = {"conv_a": jnp.stack(conv_a), "ssm_re": jnp.stack(ssm_re), "ssm_im": jnp.stack(ssm_im),
                 "logf": jnp.swapaxes(jnp.stack(lfs), 2, 3), "ffn": jnp.stack(ffs)}
    if prompt:
        for name, stack in zip(("k", "v"), kv_stacks):
            new_state[name] = jnp.transpose(stack.reshape(N_ODD, B, C_HEADS, C_HEAD_DIM, L), (0, 1, 4, 2, 3))
    else:
        new_state.update({"k": jnp.stack(ks), "v": jnp.stack(vs), "gmlp_v": jnp.stack(gvs)})
    return h, new_state


def kernel(x_prompt, x_sample, p_prompt, p_sample, cache_conv_a, state_ssm_re, state_ssm_im, cache_k, cache_v, cache_logf, cache_ffn_conv, g_mix_pre, g_mix_post, g_ffn_pre, g_ffn_post, w_even_in, w_conv_a, s5_a_re, s5_a_im, s5_log_dt, s5_b_re, s5_b_im, s5_c_re, s5_c_im, s5_d, w_glu, w_even_out, w_odd_in, b_forget, w_spatial, b_spatial, g_gmlp_v, w_odd_out, w_ffn_up, w_ffn_conv, w_ffn_down, w_ple, w_ple_gate):
    W = {"g_mix_pre": g_mix_pre, "g_mix_post": g_mix_post, "g_ffn_pre": g_ffn_pre, "g_ffn_post": g_ffn_post,
         "w_even_in": w_even_in, "w_conv_a": w_conv_a, "s5_a_re": s5_a_re, "s5_a_im": s5_a_im,
         "s5_log_dt": s5_log_dt, "s5_b_re": s5_b_re, "s5_b_im": s5_b_im, "s5_c_re": s5_c_re,
         "s5_c_im": s5_c_im, "s5_d": s5_d, "w_glu": w_glu, "w_even_out": w_even_out,
         "w_odd_in": w_odd_in, "b_forget": b_forget, "w_spatial": w_spatial, "b_spatial": b_spatial,
         "g_gmlp_v": g_gmlp_v, "w_odd_out": w_odd_out, "w_ffn_up": w_ffn_up, "w_ffn_conv": w_ffn_conv,
         "w_ffn_down": w_ffn_down, "w_ple": w_ple, "w_ple_gate": w_ple_gate}
    LW = [_layer_params(i, W) for i in range(DEPTH)]
    y_prompt, sp = _run_trunk(x_prompt, p_prompt, None, LW, bB=1, tT=512, tT_s5=32, tq=256, tT_ffn=512)
    dec_b, dec_l, _ = x_sample.shape
    past = cache_k.shape[2]
    st = {"conv_a": cache_conv_a, "ssm_re": state_ssm_re, "ssm_im": state_ssm_im, "ffn": cache_ffn_conv,
          "k_t": jnp.transpose(cache_k, (0, 1, 3, 4, 2)).reshape(N_ODD, dec_b, C_WIDTH, past),
          "v_t": jnp.transpose(cache_v, (0, 1, 3, 4, 2)).reshape(N_ODD, dec_b, C_WIDTH, past),
          "logf_t": jnp.swapaxes(cache_logf, 2, 3)}
    y_sample, ss = _run_trunk(x_sample, p_sample, st, LW, bB=dec_b, tT=dec_l, tT_s5=dec_l, tq=dec_l, tT_ffn=dec_l)
    return (y_prompt, y_sample,
            sp["conv_a"], sp["ssm_re"], sp["ssm_im"], sp["k"], sp["v"], sp["logf"], sp["ffn"],
            ss["conv_a"], ss["ssm_re"], ss["ssm_im"], ss["k"], ss["v"], ss["logf"], ss["gmlp_v"], ss["ffn"])
```

```python
import functools

import jax
import jax.numpy as jnp
import numpy as np
from jax import lax
from jax.experimental import pallas as pl
from jax.experimental.pallas import tpu as pltpu

F32 = jnp.float32
BF16 = jnp.bfloat16

D_MODEL = 1024
DEPTH = 4
N_ODD = DEPTH // 2
PLE_DIM = 256
RMS_EPS = 1e-6
A_WIDTH = 512
A_CONV = 3
B_WIDTH = 512
S5_GROUP = 16
S5_GROUPS = 32
S5_STATE = 64
S5_PAIRS = S5_GROUPS // 2
S5_QUADS = B_WIDTH // 128
C_HEADS = 8
C_HEAD_DIM = 64
C_WIDTH = 512
HEAD_PAIRS = C_HEADS // 2
D_WIDTH = 512
GMLP_CHUNK = 128
D_FF = 2816
FFN_CONV = 3
FF_CHUNK = 256
N_FF_CHUNKS = D_FF // FF_CHUNK
MASK_VALUE = -1e30
CONV_PAD = 8
HIST = A_CONV - 1

V7X_VMEM_LIMIT = 56 * 1024 * 1024
LANES = 128
LOG2E = 1.4426950408889634


def _rms(x, g):
    return x * lax.rsqrt(jnp.mean(x * x, axis=-1, keepdims=True) + RMS_EPS) * g


GELU_A = 0.7978845608028654
GELU_B = GELU_A * 0.044715


def _gelu_tanh(x):
    half = 0.5 * x
    return half + half * jnp.tanh(x * (GELU_A + GELU_B * (x * x)))


def _dot(a, b):
    return jnp.dot(a, b, preferred_element_type=F32)


def _dot_nt(a, b):
    return lax.dot_general(a, b, (((1,), (1,)), ((), ())), preferred_element_type=F32)


def _cumsum_lanes(x, seg):
    pos = lax.broadcasted_iota(jnp.int32, x.shape, 1)
    if seg != x.shape[-1]:
        pos = pos % seg
    shift = 1
    while shift < seg:
        x = x + jnp.where(pos >= shift, pltpu.roll(x, shift, axis=1), 0.0)
        shift *= 2
    return x


def _const_spec(shape):
    zeros = (0,) * len(shape)
    return pl.BlockSpec(shape, lambda *_: zeros, pipeline_mode=pl.Buffered(1))


def _params(*sem):
    return pltpu.CompilerParams(dimension_semantics=sem, vmem_limit_bytes=V7X_VMEM_LIMIT)


def _causal_conv3(cb_ref, cur, hist, w, tT):
    if cur.shape[0] == 1:
        x = cur[0]
        h = hist[0]
        row = lax.broadcasted_iota(jnp.int32, (8, x.shape[-1]), 0)
        s1 = pltpu.roll(x, 1, axis=0)
        s2 = pltpu.roll(x, 2, axis=0)
        head1 = jnp.where(row == 0, h[1:2], s1[0:8])
        head2 = jnp.where(row == 0, h[0:1], jnp.where(row == 1, h[1:2], s2[0:8]))
        s1 = jnp.concatenate([head1, s1[8:]], axis=0)
        s2 = jnp.concatenate([head2, s2[8:]], axis=0)
        y = w[0:1] * s2 + w[1:2] * s1 + w[2:3] * x
        return y[None], x[tT - HIST:tT][None]
    cb_ref[:, CONV_PAD - HIST:CONV_PAD, :] = hist
    cb_ref[:, CONV_PAD:CONV_PAD + tT, :] = cur
    y = w[0:1][None] * cb_ref[:, CONV_PAD - 2:CONV_PAD - 2 + tT, :]
    y = y + w[1:2][None] * cb_ref[:, CONV_PAD - 1:CONV_PAD - 1 + tT, :]
    y = y + w[2:3][None] * cur
    return y, cb_ref[:, CONV_PAD + tT - HIST:CONV_PAD + tT, :]


def _even_in_kernel(h_ref, g_ref, w_ref, wc_ref, hist_ref, ya_ref, u_ref, nh_ref, cb_ref, *, bB, tT):
    @pl.when(pl.program_id(1) == 0)
    def _():
        nh_ref[...] = hist_ref[...]

    x = h_ref[...].reshape(bB * tT, D_MODEL)
    hn = _rms(x, g_ref[...])
    z = _dot(hn.astype(BF16), w_ref[...])
    gb = z[:, 0:A_WIDTH]
    gc = z[:, A_WIDTH:2 * A_WIDTH]
    xa = z[:, 2 * A_WIDTH:3 * A_WIDTH]
    u = z[:, 3 * A_WIDTH:]
    cx = (gc * xa).reshape(bB, tT, A_WIDTH)
    yc, last = _causal_conv3(cb_ref, cx, nh_ref[...], wc_ref[...], tT)
    nh_ref[...] = last
    ya_ref[...] = (gb.reshape(bB, tT, A_WIDTH) * yc).astype(BF16)
    for b in range(bB):
        u_ref[:, b * B_WIDTH:(b + 1) * B_WIDTH] = u[b * tT:(b + 1) * tT]


def _even_in(h, g, w, wc, hist, bB, tT):
    B, L, _ = h.shape
    grid = (B // bB, L // tT)
    kern = functools.partial(_even_in_kernel, bB=bB, tT=tT)
    return pl.pallas_call(
        kern,
        grid=grid,
        in_specs=[
            pl.BlockSpec((bB, tT, D_MODEL), lambda b, t: (b, t, 0)),
            _const_spec((1, D_MODEL)),
            _const_spec((D_MODEL, 4 * A_WIDTH)),
            _const_spec((A_CONV, A_WIDTH)),
            pl.BlockSpec((bB, HIST, A_WIDTH), lambda b, t: (b, 0, 0)),
        ],
        out_specs=[
            pl.BlockSpec((bB, tT, A_WIDTH), lambda b, t: (b, t, 0)),
            pl.BlockSpec((tT, bB * B_WIDTH), lambda b, t: (t, b)),
            pl.BlockSpec((bB, HIST, A_WIDTH), lambda b, t: (b, 0, 0)),
        ],
        out_shape=[
            jax.ShapeDtypeStruct((B, L, A_WIDTH), BF16),
            jax.ShapeDtypeStruct((L, B * B_WIDTH), F32),
            jax.ShapeDtypeStruct((B, HIST, A_WIDTH), F32),
        ],
        scratch_shapes=[pltpu.VMEM((bB, tT + CONV_PAD, A_WIDTH), F32)],
        compiler_params=_params("arbitrary", "arbitrary"),
        name="even_in",
    )(h, g, w, wc, hist)


def _s5_kernel(u_ref, are_ref, aim_ref, ldt_ref, bre_ref, bim_ref, cre_ref, cim_ref, d_ref, wglu_ref,
               h0r_ref, h0i_ref, yb_ref, hr_ref, hi_ref, lam_ref, wb_ref, wc_ref, sre_ref, sim_ref, *, B, tT):
    M = tT * B

    @pl.when(pl.program_id(0) == 0)
    def _():
        a_re = are_ref[...]
        a_im = aim_ref[...]
        dt = jnp.exp(ldt_ref[...])
        mag = jnp.exp(a_re * dt)
        ab_re = mag * jnp.cos(a_im * dt)
        ab_im = mag * jnp.sin(a_im * dt)
        z_re = ab_re - 1.0
        den = a_re * a_re + a_im * a_im
        f_re = (z_re * a_re + ab_im * a_im) / den
        f_im = (ab_im * a_re - z_re * a_im) / den
        lam_ref[0] = ab_re
        lam_ref[1] = ab_im
        for j in range(S5_PAIRS):
            fr = f_re[j:j + 1]
            fi = f_im[j:j + 1]
            br = bre_ref[j]
            bi = bim_ref[j]
            wb_ref[j, :, 0:LANES] = (fr * br - fi * bi).astype(BF16)
            wb_ref[j, :, LANES:2 * LANES] = (fr * bi + fi * br).astype(BF16)
            wc_ref[j, 0:LANES, :] = cre_ref[j].astype(BF16)
            wc_ref[j, LANES:2 * LANES, :] = (-cim_ref[j]).astype(BF16)
        hr_ref[...] = h0r_ref[...]
        hi_ref[...] = h0i_ref[...]

    u = u_ref[...].reshape(M, B_WIDTH)
    ub = u.astype(BF16)
    d = d_ref[...]
    per_quad = S5_PAIRS // S5_QUADS
    cols = [None] * S5_QUADS

    def b_proj(q):
        for j in range(q * per_quad, (q + 1) * per_quad):
            bu = _dot(ub[:, q * LANES:(q + 1) * LANES], wb_ref[j])
            sre_ref[j] = bu[:, 0:LANES]
            sim_ref[j] = bu[:, LANES:2 * LANES]

    def scan(q):
        pairs = range(q * per_quad, (q + 1) * per_quad)
        lr = [lam_ref[0, j:j + 1, :] for j in pairs]
        li = [lam_ref[1, j:j + 1, :] for j in pairs]
        hr = [hr_ref[:, j * LANES:(j + 1) * LANES] for j in pairs]
        hi = [hi_ref[:, j * LANES:(j + 1) * LANES] for j in pairs]
        for t in range(tT):
            rows = slice(t * B, (t + 1) * B)
            for k, j in enumerate(pairs):
                r = lr[k] * hr[k] - li[k] * hi[k] + sre_ref[j, rows, :]
                i = lr[k] * hi[k] + li[k] * hr[k] + sim_ref[j, rows, :]
                sre_ref[j, rows, :] = r
                sim_ref[j, rows, :] = i
                hr[k], hi[k] = r, i
        for k, j in enumerate(pairs):
            hr_ref[:, j * LANES:(j + 1) * LANES] = hr[k]
            hi_ref[:, j * LANES:(j + 1) * LANES] = hi[k]

    def c_proj(q):
        acc = None
        for j in range(q * per_quad, (q + 1) * per_quad):
            hs = jnp.concatenate([sre_ref[j], sim_ref[j]], axis=-1).astype(BF16)
            part = _dot(hs, wc_ref[j])
            acc = part if acc is None else acc + part
        cols[q] = acc + d[:, q * LANES:(q + 1) * LANES] * u[:, q * LANES:(q + 1) * LANES]

    b_proj(0)
    for q in range(S5_QUADS):
        if q + 1 < S5_QUADS:
            b_proj(q + 1)
        scan(q)
        if q >= 1:
            c_proj(q - 1)
    c_proj(S5_QUADS - 1)
    y = jnp.concatenate(cols, axis=-1)
    g = _gelu_tanh(y)
    out = g * jax.nn.sigmoid(_dot(g.astype(BF16), wglu_ref[...]))
    yb_ref[...] = out.reshape(tT, B, B_WIDTH)


def _s5(u_tm, prm, h0r, h0i, B, L, tT):
    M = tT * B
    state = S5_PAIRS * LANES
    kern = functools.partial(_s5_kernel, B=B, tT=tT)
    tile = lambda i: (i, 0, 0)
    return pl.pallas_call(
        kern,
        grid=(L // tT,),
        in_specs=[
            pl.BlockSpec((tT, B, B_WIDTH), tile),
            _const_spec((S5_PAIRS, LANES)),
            _const_spec((S5_PAIRS, LANES)),
            _const_spec((S5_PAIRS, LANES)),
            _const_spec((S5_PAIRS, LANES, LANES)),
            _const_spec((S5_PAIRS, LANES, LANES)),
            _const_spec((S5_PAIRS, LANES, LANES)),
            _const_spec((S5_PAIRS, LANES, LANES)),
            _const_spec((1, B_WIDTH)),
            _const_spec((B_WIDTH, B_WIDTH)),
            _const_spec((B, state)),
            _const_spec((B, state)),
        ],
        out_specs=[
            pl.BlockSpec((tT, B, B_WIDTH), tile),
            pl.BlockSpec((B, state), lambda i: (0, 0)),
            pl.BlockSpec((B, state), lambda i: (0, 0)),
        ],
        out_shape=[
            jax.ShapeDtypeStruct((L, B, B_WIDTH), F32),
            jax.ShapeDtypeStruct((B, state), F32),
            jax.ShapeDtypeStruct((B, state), F32),
        ],
        scratch_shapes=[
            pltpu.VMEM((2, S5_PAIRS, LANES), F32),
            pltpu.VMEM((S5_PAIRS, LANES, 2 * LANES), BF16),
            pltpu.VMEM((S5_PAIRS, 2 * LANES, LANES), BF16),
            pltpu.VMEM((S5_PAIRS, M, LANES), F32),
            pltpu.VMEM((S5_PAIRS, M, LANES), F32),
        ],
        compiler_params=_params("arbitrary"),
        name="s5_scan",
    )(u_tm.reshape(L, B, B_WIDTH), prm["a_re"], prm["a_im"], prm["ldt"], prm["b_re"], prm["b_im"],
      prm["c_re"], prm["c_im"], prm["d"], prm["w_glu"], h0r, h0i)


def _odd_in_kernel(*refs, bB, tT, lc, kv_channel_major, n_alias):
    (h_ref, g_ref, w_ref, wkv_ref, wft_ref, bft_ref, gv_ref, ws_ref, bs_ref) = refs[:9]
    outs = refs[9 + n_alias:]
    if kv_channel_major:
        q_ref, k_ref, v_ref, kb_ref, vb_ref, lf_ref, ft_ref, yd_ref, carry_ref = outs
        vn_ref = None
    else:
        q_ref, k_ref, v_ref, kb_ref, vb_ref, lf_ref, ft_ref, yd_ref, vn_ref, carry_ref = outs
    M = bB * tT

    @pl.when(pl.program_id(1) == 0)
    def _():
        carry_ref[...] = jnp.zeros_like(carry_ref)

    x = h_ref[...].reshape(M, D_MODEL)
    hn = _rms(x, g_ref[...]).astype(BF16)
    z = _dot(hn, w_ref[...])
    q = z[:, 0:C_WIDTH] * (C_HEAD_DIM ** -0.5 * LOG2E)
    u = z[:, C_WIDTH:C_WIDTH + D_WIDTH]
    vd = z[:, C_WIDTH + D_WIDTH:C_WIDTH + 2 * D_WIDTH]
    q_ref[...] = q.reshape(bB, tT, C_WIDTH).astype(BF16)
    if kv_channel_major:
        kv = _dot_nt(wkv_ref[...], hn)
        k_ref[0, 0] = kv[0:C_WIDTH]
        v_ref[0, 0] = kv[C_WIDTH:]
        kb_ref[0] = kv[0:C_WIDTH].astype(BF16)
        ones = jnp.ones((C_HEAD_DIM, M), BF16)
        for hh in range(C_HEADS):
            v_at = hh * LANES + (hh % 2) * C_HEAD_DIM
            ones_at = hh * LANES + (1 - hh % 2) * C_HEAD_DIM
            vb_ref[0, v_at:v_at + C_HEAD_DIM, :] = (
                kv[C_WIDTH + hh * C_HEAD_DIM:C_WIDTH + (hh + 1) * C_HEAD_DIM].astype(BF16))
            vb_ref[0, ones_at:ones_at + C_HEAD_DIM, :] = ones
    else:
        k = z[:, C_WIDTH + 2 * D_WIDTH:2 * C_WIDTH + 2 * D_WIDTH].reshape(bB, tT, C_WIDTH)
        v = z[:, 2 * C_WIDTH + 2 * D_WIDTH:].reshape(bB, tT, C_WIDTH)
        k_ref[...] = k
        v_ref[...] = v
        kb_ref[...] = k.astype(BF16)
        vb_ref[...] = v.astype(BF16)

    logf_t = jax.nn.log_sigmoid(_dot_nt(wft_ref[...], hn) + bft_ref[...])
    ft = _cumsum_lanes(logf_t, tT) + carry_ref[...]
    for b in range(bB):
        lf_ref[b] = logf_t[:, b * tT:(b + 1) * tT]
        ft_ref[b] = ft[:, b * tT:(b + 1) * tT]
    if bB == 1:
        carry_ref[...] = ft[:, M - 1:M]

    vn = _rms(vd, gv_ref[...])
    if vn_ref is not None:
        vn_ref[...] = vn.reshape(bB, tT, D_WIDTH)
    vnb = vn.astype(BF16)
    rr = lax.broadcasted_iota(jnp.int32, (lc, lc), 0)
    cc = lax.broadcasted_iota(jnp.int32, (lc, lc), 1)
    wsm = [jnp.where(cc <= rr, ws_ref[hh], 0.0).astype(BF16) for hh in range(C_HEADS)]
    first_head = lax.broadcasted_iota(jnp.int32, (lc, LANES), 1) < C_HEAD_DIM
    bias = bs_ref[...]
    for ch in range(M // lc):
        rows = slice(ch * lc, (ch + 1) * lc)
        cols = []
        for hp in range(HEAD_PAIRS):
            vp = vnb[rows, hp * LANES:(hp + 1) * LANES]
            cols.append(jnp.where(first_head, _dot(wsm[2 * hp], vp), _dot(wsm[2 * hp + 1], vp)))
        mixed = jnp.concatenate(cols, axis=-1) + bias
        b, off = divmod(ch * lc, tT)
        yd_ref[b, off:off + lc, :] = (u[rows] * mixed).astype(BF16)


def _odd_in(h, lw, bB, tT, j, kv_stacks):
    B, L, _ = h.shape
    nT = L // tT
    assert bB == 1 or nT == 1
    kv_channel_major = kv_stacks is not None
    lc = lw["w_s"].shape[-1]
    tile = lambda b, t: (b, t, 0)
    chan = lambda b, t: (b, 0, t)

    def act(width, dtype):
        return pl.BlockSpec((bB, tT, width), tile), jax.ShapeDtypeStruct((B, L, width), dtype)

    heads = (pl.BlockSpec((bB, C_HEADS, tT), chan), jax.ShapeDtypeStruct((B, C_HEADS, L), F32))
    if kv_channel_major:
        assert bB == 1
        stack = (pl.BlockSpec((1, 1, C_WIDTH, tT), lambda b, t: (j, b, 0, t)),
                 jax.ShapeDtypeStruct((N_ODD, B, C_WIDTH, L), F32))
        chan_bf = (pl.BlockSpec((1, C_WIDTH, tT), chan), jax.ShapeDtypeStruct((B, C_WIDTH, L), BF16))
        chan_v = (pl.BlockSpec((1, C_HEADS * LANES, tT), chan), jax.ShapeDtypeStruct((B, C_HEADS * LANES, L), BF16))
        outs = [act(C_WIDTH, BF16), stack, stack, chan_bf, chan_v, heads, heads, act(D_WIDTH, BF16)]
        w_kv_spec = _const_spec((2 * C_WIDTH, D_MODEL))
    else:
        outs = [act(C_WIDTH, BF16), act(C_WIDTH, F32), act(C_WIDTH, F32), act(C_WIDTH, BF16), act(C_WIDTH, BF16),
                heads, heads, act(D_WIDTH, BF16), act(D_WIDTH, F32)]
        w_kv_spec = _const_spec((1, 1))
    alias_in = list(kv_stacks) if kv_channel_major else []
    n_fixed = 9
    kern = functools.partial(_odd_in_kernel, bB=bB, tT=tT, lc=lc, kv_channel_major=kv_channel_major,
                             n_alias=len(alias_in))
    w_in = lw["w_in_cm"] if kv_channel_major else lw["w_in_rm"]
    w_kv = lw["w_kv_t"] if kv_channel_major else jnp.zeros((1, 1), BF16)
    return pl.pallas_call(
        kern,
        grid=(B // bB, nT),
        in_specs=[
            pl.BlockSpec((bB, tT, D_MODEL), tile),
            _const_spec((1, D_MODEL)),
            _const_spec(w_in.shape),
            w_kv_spec,
            _const_spec((C_HEADS, D_MODEL)),
            _const_spec((C_HEADS, 1)),
            _const_spec((1, D_WIDTH)),
            _const_spec((C_HEADS, lc, lc)),
            _const_spec((lc, D_WIDTH)),
        ] + [pl.BlockSpec(memory_space=pl.ANY)] * len(alias_in),
        out_specs=[o[0] for o in outs],
        out_shape=[o[1] for o in outs],
        input_output_aliases={n_fixed + i: 1 + i for i in range(len(alias_in))},
        scratch_shapes=[pltpu.VMEM((C_HEADS, 1), F32)],
        compiler_params=_params("arbitrary", "arbitrary"),
        name="odd_in",
    )(h, lw["g_pre"], w_in, w_kv, lw["w_ft"], lw["b_ft"], lw["g_v"], lw["w_s"], lw["b_s"], *alias_in)


def _attn_prompt_kernel(qi_ref, kj_ref, q_ref, kt_ref, vt_ref, ft_ref, o_ref, m_ref, acc_ref, *, tq):
    p = pl.program_id(1)
    qi = qi_ref[p]
    kj = kj_ref[p]
    first_head = lax.broadcasted_iota(jnp.int32, (tq, LANES), 1) < C_HEAD_DIM

    @pl.when(kj == 0)
    def _():
        m_ref[...] = jnp.full_like(m_ref, -jnp.inf)
        acc_ref[...] = jnp.zeros_like(acc_ref)

    def block(diagonal):
        if diagonal:
            visible = (lax.broadcasted_iota(jnp.int32, (tq, tq), 1)
                       <= lax.broadcasted_iota(jnp.int32, (tq, tq), 0))
        fk = ft_ref[0] * LOG2E
        for hp in range(HEAD_PAIRS):
            lanes = slice(hp * LANES, (hp + 1) * LANES)
            qp = q_ref[0, :, lanes]
            kt = kt_ref[0, lanes, :]
            qs = (jnp.where(first_head, qp, jnp.zeros_like(qp)), jnp.where(first_head, jnp.zeros_like(qp), qp))
            done = []
            for e in range(2):
                h = 2 * hp + e
                s = _dot(qs[e], kt) - fk[h:h + 1]
                if diagonal:
                    s = jnp.where(visible, s, MASK_VALUE)
                m_old = m_ref[h]
                m_new = jnp.maximum(m_old, jnp.max(s, axis=-1, keepdims=True))
                a = jnp.exp2(m_old - m_new)
                pr = jnp.exp2(s - jnp.concatenate([m_new] * (tq // LANES), axis=-1))
                acc = acc_ref[h] * a + _dot_nt(pr.astype(BF16), vt_ref[0, h * LANES:(h + 1) * LANES, :])
                if diagonal:
                    done.append(acc)
                else:
                    m_ref[h] = m_new
                    acc_ref[h] = acc
            if diagonal:
                num = jnp.where(first_head, done[0], done[1])
                den = pltpu.roll(jnp.where(first_head, done[1], done[0]), C_HEAD_DIM, axis=1)
                o_ref[0, :, lanes] = (num / den).astype(BF16)

    @pl.when(kj < qi)
    def _():
        block(False)

    @pl.when(kj == qi)
    def _():
        block(True)


def _attn_prompt(q, ktb, vtb, ft, tq):
    B, L, _ = q.shape
    nq = L // tq
    pairs = [(i, k) for i in range(nq) for k in range(i + 1)]
    qi = jnp.asarray(np.array([p[0] for p in pairs], np.int32))
    kj = jnp.asarray(np.array([p[1] for p in pairs], np.int32))
    kern = functools.partial(_attn_prompt_kernel, tq=tq)
    q_spec = pl.BlockSpec((1, tq, C_WIDTH), lambda b, p, qi, kj: (b, qi[p], 0))
    return pl.pallas_call(
        kern,
        grid_spec=pltpu.PrefetchScalarGridSpec(
            num_scalar_prefetch=2,
            grid=(B, len(pairs)),
            in_specs=[q_spec,
                      pl.BlockSpec((1, C_WIDTH, tq), lambda b, p, qi, kj: (b, 0, kj[p])),
                      pl.BlockSpec((1, C_HEADS * LANES, tq), lambda b, p, qi, kj: (b, 0, kj[p])),
                      pl.BlockSpec((1, C_HEADS, tq), lambda b, p, qi, kj: (b, 0, kj[p]))],
            out_specs=q_spec,
            scratch_shapes=[pltpu.VMEM((C_HEADS, tq, LANES), F32), pltpu.VMEM((C_HEADS, tq, LANES), F32)],
        ),
        out_shape=jax.ShapeDtypeStruct((B, L, C_WIDTH), BF16),
        compiler_params=_params("arbitrary", "arbitrary"),
        name="attn_prompt",
    )(qi, kj, q, ktb, vtb, ft)


def _attn_sample_kernel(q_ref, kn_ref, vn_ref, kpt_ref, vpt_ref, lpt_ref, ft_ref, o_ref, *, L, P):
    first_head = lax.broadcasted_iota(jnp.int32, (L, LANES), 1) < C_HEAD_DIM
    past_cum = _cumsum_lanes(lpt_ref[0, 0], P)
    past_after = (past_cum[:, P - 1:P] - past_cum) * LOG2E
    causal = lax.broadcasted_iota(jnp.int32, (L, L), 1) <= lax.broadcasted_iota(jnp.int32, (L, L), 0)
    ft = ft_ref[0] * LOG2E
    for hp in range(HEAD_PAIRS):
        lanes = slice(hp * LANES, (hp + 1) * LANES)
        qp = q_ref[0, :, lanes]
        qs = (jnp.where(first_head, qp, jnp.zeros_like(qp)), jnp.where(first_head, jnp.zeros_like(qp), qp))
        kpt = kpt_ref[0, 0, lanes, :].astype(BF16)
        vpt = vpt_ref[0, 0, lanes, :].astype(BF16)
        kn = kn_ref[0, :, lanes]
        vn = vn_ref[0, :, lanes]
        outs = []
        for e in range(2):
            h = 2 * hp + e
            sp = _dot(qs[e], kpt) + past_after[h:h + 1]
            sn = jnp.where(causal, _dot_nt(qs[e], kn) - ft[h:h + 1], MASK_VALUE)
            m = jnp.maximum(jnp.max(sp, axis=-1, keepdims=True), jnp.max(sn, axis=-1, keepdims=True))
            pp = jnp.exp2(sp - m)
            pn = jnp.exp2(sn - m)
            den = jnp.sum(pp, axis=-1, keepdims=True) + jnp.sum(pn, axis=-1, keepdims=True)
            outs.append((_dot_nt(pp.astype(BF16), vpt) + _dot(pn.astype(BF16), vn)) / den)
        o_ref[0, :, lanes] = jnp.where(first_head, outs[0], outs[1]).astype(BF16)


def _attn_sample(q, kb, vb, k_past_t, v_past_t, logf_past_t, ft, j):
    B, L, _ = q.shape
    P = k_past_t.shape[-1]
    kern = functools.partial(_attn_sample_kernel, L=L, P=P)
    new = pl.BlockSpec((1, L, C_WIDTH), lambda b: (b, 0, 0))
    past = pl.BlockSpec((1, 1, C_WIDTH, P), lambda b: (j, b, 0, 0))
    return pl.pallas_call(
        kern,
        grid=(B,),
        in_specs=[new, new, new, past, past,
                  pl.BlockSpec((1, 1, C_HEADS, P), lambda b: (j, b, 0, 0)),
                  pl.BlockSpec((1, C_HEADS, L), lambda b: (b, 0, 0))],
        out_specs=new,
        out_shape=jax.ShapeDtypeStruct((B, L, C_WIDTH), BF16),
        compiler_params=_params("arbitrary"),
        name="attn_sample",
    )(q, kb, vb, k_past_t, v_past_t, logf_past_t, ft)


def _ffn_kernel(h_ref, a_ref, b_ref, wo_ref, gpost_ref, gpre_ref, wup_ref, wcv_ref, fh_ref, wdn_ref, gfp_ref,
                wgate_ref, pe_ref, wple_ref, o_ref, nf_ref, cb_ref, act_ref, *, bB, tT, b_time_major):
    M = bB * tT

    @pl.when(pl.program_id(1) == 0)
    def _():
        nf_ref[...] = fh_ref[0]

    a = a_ref[...].reshape(M, a_ref.shape[-1])
    if b_time_major:
        width = b_ref.shape[-1] // bB
        bpart = jnp.concatenate([b_ref[:, i * width:(i + 1) * width] for i in range(bB)], axis=0)
    else:
        bpart = b_ref[...].reshape(M, b_ref.shape[-1])
    ka = a.shape[-1]
    y = _dot(a.astype(BF16), wo_ref[0:ka, :]) + _dot(bpart.astype(BF16), wo_ref[ka:, :])
    h1 = h_ref[...].reshape(M, D_MODEL) + _rms(y, gpost_ref[...])
    hn = _rms(h1, gpre_ref[...]).astype(BF16)

    def up_proj(c):
        return [_dot(hn, wup_ref[:, col0:col0 + FF_CHUNK]).reshape(bB, tT, FF_CHUNK)
                for col0 in (c * FF_CHUNK, D_FF + c * FF_CHUNK)]

    ups = up_proj(0)
    for c in range(N_FF_CHUNKS):
        nxt = up_proj(c + 1) if c + 1 < N_FF_CHUNKS else None
        cb = cb_ref.at[c % 2]
        halves = []
        for half, col0 in enumerate((c * FF_CHUNK, D_FF + c * FF_CHUNK)):
            cols = slice(col0, col0 + FF_CHUNK)
            uc, last = _causal_conv3(cb.at[:, :, half * FF_CHUNK:(half + 1) * FF_CHUNK],
                                     ups[half], nf_ref[:, :, cols], wcv_ref[:, cols], tT)
            nf_ref[:, :, cols] = last
            halves.append(uc)
        act_ref[:, c * FF_CHUNK:(c + 1) * FF_CHUNK] = (
            (_gelu_tanh(halves[0]) * halves[1]).reshape(M, FF_CHUNK).astype(BF16))
        ups = nxt

    h2 = h1 + _rms(_dot(act_ref[...], wdn_ref[...]), gfp_ref[...])
    gate = jax.nn.sigmoid(_dot(h2.astype(BF16), wgate_ref[...]))
    pe = pe_ref[0].reshape(M, PLE_DIM).astype(BF16)
    h3 = h2 + gate * _dot(pe, wple_ref[...])
    o_ref[...] = h3.reshape(bB, tT, D_MODEL)


def _ffn(h, a, b, b_time_major, lw, fh, pe, layer, bB, tT):
    B, L, _ = h.shape
    grid = (B // bB, L // tT)
    kern = functools.partial(_ffn_kernel, bB=bB, tT=tT, b_time_major=b_time_major)
    tile = lambda bi, t: (bi, t, 0)
    wa = a.shape[-1]
    if b_time_major:
        wb = b.shape[-1] // B
        b_spec = pl.BlockSpec((tT, bB * wb), lambda bi, t: (t, bi))
    else:
        wb = b.shape[-1]
        b_spec = pl.BlockSpec((bB, tT, wb), tile)
    fslot = layer if fh.shape[0] > 1 else 0
    return pl.pallas_call(
        kern,
        grid=grid,
        in_specs=[
            pl.BlockSpec((bB, tT, D_MODEL), tile),
            pl.BlockSpec((bB, tT, wa), tile),
            b_spec,
            _const_spec((wa + wb, D_MODEL)),
            _const_spec((1, D_MODEL)),
            _const_spec((1, D_MODEL)),
            _const_spec((D_MODEL, 2 * D_FF)),
            _const_spec((FFN_CONV, 2 * D_FF)),
            pl.BlockSpec((1, bB, HIST, 2 * D_FF), lambda bi, t: (fslot, bi, 0, 0)),
            _const_spec((D_FF, D_MODEL)),
            _const_spec((1, D_MODEL)),
            _const_spec((D_MODEL, D_MODEL)),
            pl.BlockSpec((1, bB, tT, PLE_DIM), lambda bi, t: (layer, bi, t, 0)),
            _const_spec((PLE_DIM, D_MODEL)),
        ],
        out_specs=[pl.BlockSpec((bB, tT, D_MODEL), tile),
                   pl.BlockSpec((bB, HIST, 2 * D_FF), lambda bi, t: (bi, 0, 0))],
        out_shape=[
            jax.ShapeDtypeStruct((B, L, D_MODEL), F32),
            jax.ShapeDtypeStruct((B, HIST, 2 * D_FF), F32),
        ],
        scratch_shapes=[
            pltpu.VMEM((2, bB, tT + CONV_PAD, 2 * FF_CHUNK), F32),
            pltpu.VMEM((bB * tT, D_FF), BF16),
        ],
        compiler_params=_params("arbitrary", "arbitrary"),
        name="mix_out_ffn",
    )(h, a, b, lw["w_out"], lw["g_post"], lw["g_ffn_pre"], lw["w_up"], lw["w_fconv"], fh, lw["w_down"],
      lw["g_ffn_post"], lw["w_gate"], pe, lw["w_ple"])


def _s5_block_layout(x, state_major):
    per_quad = S5_PAIRS // S5_QUADS
    eye2 = jnp.eye(2, dtype=x.dtype)
    slot = jax.nn.one_hot(jnp.arange(S5_PAIRS) % per_quad, per_quad, dtype=x.dtype)
    if state_major:
        x4 = x.reshape(S5_PAIRS, 2, S5_STATE, S5_GROUP)
        blk = jnp.einsum("jgpc,gh->jgchp", x4, eye2).reshape(S5_PAIRS, 2 * S5_GROUP, LANES)
        return jnp.einsum("jkn,jr->jrkn", blk, slot).reshape(S5_PAIRS, LANES, LANES)
    x4 = x.reshape(S5_PAIRS, 2, S5_GROUP, S5_STATE)
    blk = jnp.einsum("jgcp,gh->jgphc", x4, eye2).reshape(S5_PAIRS, LANES, 2 * S5_GROUP)
    return jnp.einsum("jnk,jr->jnrk", blk, slot).reshape(S5_PAIRS, LANES, LANES)


def _layer_params(i, W):
    j = i // 2
    lw = {
        "g_pre": W["g_mix_pre"][i][None], "g_post": W["g_mix_post"][i][None],
        "g_ffn_pre": W["g_ffn_pre"][i][None], "g_ffn_post": W["g_ffn_post"][i][None],
        "w_up": W["w_ffn_up"][i].astype(BF16), "w_fconv": W["w_ffn_conv"][i],
        "w_down": W["w_ffn_down"][i].astype(BF16),
        "w_gate": W["w_ple_gate"][i].astype(BF16), "w_ple": W["w_ple"][i].astype(BF16),
    }
    if i % 2 == 0:
        lw.update({
            "w_in": W["w_even_in"][j].astype(BF16), "w_conv": W["w_conv_a"][j],
            "w_out": W["w_even_out"][j].astype(BF16),
            "s5": {
                "a_re": W["s5_a_re"][j].reshape(S5_PAIRS, LANES),
                "a_im": W["s5_a_im"][j].reshape(S5_PAIRS, LANES),
                "ldt": jnp.broadcast_to(W["s5_log_dt"][j][:, None], (S5_GROUPS, S5_STATE)).reshape(S5_PAIRS, LANES),
                "b_re": _s5_block_layout(W["s5_b_re"][j], True), "b_im": _s5_block_layout(W["s5_b_im"][j], True),
                "c_re": _s5_block_layout(W["s5_c_re"][j], False), "c_im": _s5_block_layout(W["s5_c_im"][j], False),
                "d": W["s5_d"][j].reshape(1, B_WIDTH), "w_glu": W["w_glu"][j].astype(BF16),
            },
        })
    else:
        w = W["w_odd_in"][j].astype(BF16)
        f0 = 3 * C_WIDTH
        w_q, w_kv, w_f, w_ud = w[:, :C_WIDTH], w[:, C_WIDTH:f0], w[:, f0:f0 + C_HEADS], w[:, f0 + C_HEADS:]
        lw.update({
            "w_in_cm": jnp.concatenate([w_q, w_ud], axis=1),
            "w_in_rm": jnp.concatenate([w_q, w_ud, w_kv], axis=1),
            "w_kv_t": w_kv.T, "w_ft": w_f.T, "b_ft": W["b_forget"][j][:, None],
            "g_v": W["g_gmlp_v"][j][None],
            "w_out": W["w_odd_out"][j].astype(BF16),
        })
        for lc in (GMLP_CHUNK, 32):
            lw[("w_s", lc)] = W["w_spatial"][j][:, :lc, :lc]
            lw[("b_s", lc)] = jnp.repeat(W["b_spatial"][j][:, :lc].T, D_WIDTH // C_HEADS, axis=1)
    return lw


def _run_trunk(x, pe, st, LW, bB, tT, tT_s5, tq, tT_ffn):
    B, L, _ = x.shape
    prompt = st is None
    lc = min(L, GMLP_CHUNK)
    h = x
    conv_a, ssm_re, ssm_im, ks, vs, lfs, gvs, ffs = [], [], [], [], [], [], [], []
    kv_stacks = ()
    fh = jnp.zeros((1, B, HIST, 2 * D_FF), F32) if prompt else st["ffn"]
    for i in range(DEPTH):
        j = i // 2
        lw = LW[i]
        if i % 2 == 0:
            if prompt:
                hist = jnp.zeros((B, HIST, A_WIDTH), F32)
                h0r = jnp.zeros((B, S5_GROUPS * S5_STATE), F32)
                h0i = h0r
            else:
                hist = st["conv_a"][j]
                h0r = st["ssm_re"][j].reshape(B, S5_GROUPS * S5_STATE)
                h0i = st["ssm_im"][j].reshape(B, S5_GROUPS * S5_STATE)
            ya, u_tm, new_hist = _even_in(h, lw["g_pre"], lw["w_in"], lw["w_conv"], hist, bB, tT)
            yb_tm, nr, ni = _s5(u_tm, lw["s5"], h0r, h0i, B, L, tT_s5)
            a, b, b_tm = ya, yb_tm.reshape(L, B * B_WIDTH), True
            conv_a.append(new_hist)
            ssm_re.append(nr.reshape(B, S5_GROUPS, S5_STATE))
            ssm_im.append(ni.reshape(B, S5_GROUPS, S5_STATE))
        else:
            lw = dict(lw, w_s=lw[("w_s", lc)], b_s=lw[("b_s", lc)])
            if prompt:
                q, k_stack, v_stack, ktb, vtb, logf_t, ft, yd = _odd_in(h, lw, bB, tT, j, kv_stacks)
                kv_stacks = (k_stack, v_stack)
                att = _attn_prompt(q, ktb, vtb, ft, tq)
            else:
                q, k, v, kb, vb, logf_t, ft, yd, vn = _odd_in(h, lw, bB, tT, j, None)
                att = _attn_sample(q, kb, vb, st["k_t"], st["v_t"], st["logf_t"], ft, j)
                ks.append(k.reshape(B, L, C_HEADS, C_HEAD_DIM))
                vs.append(v.reshape(B, L, C_HEADS, C_HEAD_DIM))
                gvs.append(vn)
            a, b, b_tm = att, yd, False
            lfs.append(logf_t)
        h, new_f = _ffn(h, a, b, b_tm, lw, fh, pe, i, bB, tT_ffn)
        ffs.append(new_f)
    new_state = {"conv_a": jnp.stack(conv_a), "ssm_re": jnp.stack(ssm_re), "ssm_im": jnp.stack(ssm_im),
                 "logf": jnp.swapaxes(jnp.stack(lfs), 2, 3), "ffn": jnp.stack(ffs)}
    if prompt:
        for name, stack in zip(("k", "v"), kv_stacks):
            new_state[name] = jnp.transpose(stack.reshape(N_ODD, B, C_HEADS, C_HEAD_DIM, L), (0, 1, 4, 2, 3))
    else:
        new_state.update({"k": jnp.stack(ks), "v": jnp.stack(vs), "gmlp_v": jnp.stack(gvs)})
    return h, new_state


def kernel(x_prompt, x_sample, p_prompt, p_sample, cache_conv_a, state_ssm_re, state_ssm_im, cache_k, cache_v, cache_logf, cache_ffn_conv, g_mix_pre, g_mix_post, g_ffn_pre, g_ffn_post, w_even_in, w_conv_a, s5_a_re, s5_a_im, s5_log_dt, s5_b_re, s5_b_im, s5_c_re, s5_c_im, s5_d, w_glu, w_even_out, w_odd_in, b_forget, w_spatial, b_spatial, g_gmlp_v, w_odd_out, w_ffn_up, w_ffn_conv, w_ffn_down, w_ple, w_ple_gate):
    W = {"g_mix_pre": g_mix_pre, "g_mix_post": g_mix_post, "g_ffn_pre": g_ffn_pre, "g_ffn_post": g_ffn_post,
         "w_even_in": w_even_in, "w_conv_a": w_conv_a, "s5_a_re": s5_a_re, "s5_a_im": s5_a_im,
         "s5_log_dt": s5_log_dt, "s5_b_re": s5_b_re, "s5_b_im": s5_b_im, "s5_c_re": s5_c_re,
         "s5_c_im": s5_c_im, "s5_d": s5_d, "w_glu": w_glu, "w_even_out": w_even_out,
         "w_odd_in": w_odd_in, "b_forget": b_forget, "w_spatial": w_spatial, "b_spatial": b_spatial,
         "g_gmlp_v": g_gmlp_v, "w_odd_out": w_odd_out, "w_ffn_up": w_ffn_up, "w_ffn_conv": w_ffn_conv,
         "w_ffn_down": w_ffn_down, "w_ple": w_ple, "w_ple_gate": w_ple_gate}
    LW = [_layer_params(i, W) for i in range(DEPTH)]
    y_prompt, sp = _run_trunk(x_prompt, p_prompt, None, LW, bB=1, tT=512, tT_s5=64, tq=512, tT_ffn=512)
    dec_b, dec_l, _ = x_sample.shape
    past = cache_k.shape[2]
    st = {"conv_a": cache_conv_a, "ssm_re": state_ssm_re, "ssm_im": state_ssm_im, "ffn": cache_ffn_conv,
          "k_t": jnp.transpose(cache_k, (0, 1, 3, 4, 2)).reshape(N_ODD, dec_b, C_WIDTH, past),
          "v_t": jnp.transpose(cache_v, (0, 1, 3, 4, 2)).reshape(N_ODD, dec_b, C_WIDTH, past),
          "logf_t": jnp.swapaxes(cache_logf, 2, 3)}
    y_sample, ss = _run_trunk(x_sample, p_sample, st, LW, bB=dec_b, tT=dec_l, tT_s5=dec_l, tq=dec_l, tT_ffn=dec_l)
    return (y_prompt, y_sample,
            sp["conv_a"], sp["ssm_re"], sp["ssm_im"], sp["k"], sp["v"], sp["logf"], sp["ffn"],
            ss["conv_a"], ss["ssm_re"], ss["ssm_im"], ss["k"], ss["v"], ss["logf"], ss["gmlp_v"], ss["ffn"])
```

```python
import functools

import jax
import jax.numpy as jnp
import numpy as np
from jax import lax
from jax.experimental import pallas as pl
from jax.experimental.pallas import tpu as pltpu

F32 = jnp.float32
BF16 = jnp.bfloat16

D_MODEL = 1024
DEPTH = 4
N_ODD = DEPTH // 2
PLE_DIM = 256
RMS_EPS = 1e-6
A_WIDTH = 512
A_CONV = 3
B_WIDTH = 512
S5_GROUP = 16
S5_GROUPS = 32
S5_STATE = 64
S5_PAIRS = S5_GROUPS // 2
S5_QUADS = B_WIDTH // 128
C_HEADS = 8
C_HEAD_DIM = 64
C_WIDTH = 512
HEAD_PAIRS = C_HEADS // 2
D_WIDTH = 512
GMLP_CHUNK = 128
D_FF = 2816
FFN_CONV = 3
FF_CHUNK = 256
N_FF_CHUNKS = D_FF // FF_CHUNK
MASK_VALUE = -1e30
CONV_PAD = 8
HIST = A_CONV - 1

V7X_VMEM_LIMIT = 56 * 1024 * 1024
LANES = 128
LOG2E = 1.4426950408889634


def _rms(x, g):
    return x * lax.rsqrt(jnp.mean(x * x, axis=-1, keepdims=True) + RMS_EPS) * g


GELU_A = 0.7978845608028654
GELU_B = GELU_A * 0.044715


def _gelu_tanh(x):
    half = 0.5 * x
    return half + half * jnp.tanh(x * (GELU_A + GELU_B * (x * x)))


def _dot(a, b):
    return jnp.dot(a, b, preferred_element_type=F32)


def _dot_nt(a, b):
    return lax.dot_general(a, b, (((1,), (1,)), ((), ())), preferred_element_type=F32)


def _cumsum_lanes(x, seg):
    pos = lax.broadcasted_iota(jnp.int32, x.shape, 1)
    if seg != x.shape[-1]:
        pos = pos % seg
    shift = 1
    while shift < seg:
        x = x + jnp.where(pos >= shift, pltpu.roll(x, shift, axis=1), 0.0)
        shift *= 2
    return x


def _const_spec(shape):
    zeros = (0,) * len(shape)
    return pl.BlockSpec(shape, lambda *_: zeros, pipeline_mode=pl.Buffered(1))


def _layer_spec(stack, idx):
    shape = stack.shape[1:]
    zeros = (0,) * len(shape)
    return pl.BlockSpec((None,) + shape, lambda *_: (idx,) + zeros, pipeline_mode=pl.Buffered(1))


def _params(*sem):
    return pltpu.CompilerParams(dimension_semantics=sem, vmem_limit_bytes=V7X_VMEM_LIMIT)


def _causal_conv3(cb_ref, cur, hist, w, tT):
    if cur.shape[0] == 1:
        x = cur[0]
        h = hist[0]
        row = lax.broadcasted_iota(jnp.int32, (8, x.shape[-1]), 0)
        s1 = pltpu.roll(x, 1, axis=0)
        s2 = pltpu.roll(x, 2, axis=0)
        head1 = jnp.where(row == 0, h[1:2], s1[0:8])
        head2 = jnp.where(row == 0, h[0:1], jnp.where(row == 1, h[1:2], s2[0:8]))
        s1 = jnp.concatenate([head1, s1[8:]], axis=0)
        s2 = jnp.concatenate([head2, s2[8:]], axis=0)
        y = w[0:1] * s2 + w[1:2] * s1 + w[2:3] * x
        return y[None], x[tT - HIST:tT][None]
    cb_ref[:, CONV_PAD - HIST:CONV_PAD, :] = hist
    cb_ref[:, CONV_PAD:CONV_PAD + tT, :] = cur
    y = w[0:1][None] * cb_ref[:, CONV_PAD - 2:CONV_PAD - 2 + tT, :]
    y = y + w[1:2][None] * cb_ref[:, CONV_PAD - 1:CONV_PAD - 1 + tT, :]
    y = y + w[2:3][None] * cur
    return y, cb_ref[:, CONV_PAD + tT - HIST:CONV_PAD + tT, :]


def _even_in_kernel(h_ref, g_ref, w_ref, wc_ref, hist_ref, ya_ref, u_ref, nh_ref, cb_ref, *, bB, tT):
    @pl.when(pl.program_id(1) == 0)
    def _():
        nh_ref[...] = hist_ref[...]

    x = h_ref[...].reshape(bB * tT, D_MODEL)
    hn = _rms(x, g_ref[...])
    z = _dot(hn.astype(BF16), w_ref[...])
    gb = z[:, 0:A_WIDTH]
    gc = z[:, A_WIDTH:2 * A_WIDTH]
    xa = z[:, 2 * A_WIDTH:3 * A_WIDTH]
    u = z[:, 3 * A_WIDTH:]
    cx = (gc * xa).reshape(bB, tT, A_WIDTH)
    yc, last = _causal_conv3(cb_ref, cx, nh_ref[...], wc_ref[...], tT)
    nh_ref[...] = last
    ya_ref[...] = (gb.reshape(bB, tT, A_WIDTH) * yc).astype(BF16)
    for b in range(bB):
        u_ref[:, b * B_WIDTH:(b + 1) * B_WIDTH] = u[b * tT:(b + 1) * tT]


def _even_in(h, g, w, wc, hist, bB, tT):
    B, L, _ = h.shape
    grid = (B // bB, L // tT)
    kern = functools.partial(_even_in_kernel, bB=bB, tT=tT)
    return pl.pallas_call(
        kern,
        grid=grid,
        in_specs=[
            pl.BlockSpec((bB, tT, D_MODEL), lambda b, t: (b, t, 0)),
            _const_spec((1, D_MODEL)),
            _layer_spec(*w),
            _const_spec((A_CONV, A_WIDTH)),
            pl.BlockSpec((bB, HIST, A_WIDTH), lambda b, t: (b, 0, 0)),
        ],
        out_specs=[
            pl.BlockSpec((bB, tT, A_WIDTH), lambda b, t: (b, t, 0)),
            pl.BlockSpec((tT, bB * B_WIDTH), lambda b, t: (t, b)),
            pl.BlockSpec((bB, HIST, A_WIDTH), lambda b, t: (b, 0, 0)),
        ],
        out_shape=[
            jax.ShapeDtypeStruct((B, L, A_WIDTH), BF16),
            jax.ShapeDtypeStruct((L, B * B_WIDTH), F32),
            jax.ShapeDtypeStruct((B, HIST, A_WIDTH), F32),
        ],
        scratch_shapes=[pltpu.VMEM((bB, tT + CONV_PAD, A_WIDTH), F32)],
        compiler_params=_params("arbitrary", "arbitrary"),
        name="even_in",
    )(h, g, w[0], wc, hist)


def _s5_kernel(u_ref, are_ref, aim_ref, ldt_ref, bre_ref, bim_ref, cre_ref, cim_ref, d_ref, wglu_ref,
               h0r_ref, h0i_ref, yb_ref, hr_ref, hi_ref, lam_ref, wb_ref, wc_ref, sre_ref, sim_ref, *, B, tT):
    M = tT * B

    @pl.when(pl.program_id(0) == 0)
    def _():
        a_re = are_ref[...]
        a_im = aim_ref[...]
        dt = jnp.exp(ldt_ref[...])
        mag = jnp.exp(a_re * dt)
        ab_re = mag * jnp.cos(a_im * dt)
        ab_im = mag * jnp.sin(a_im * dt)
        z_re = ab_re - 1.0
        den = a_re * a_re + a_im * a_im
        f_re = (z_re * a_re + ab_im * a_im) / den
        f_im = (ab_im * a_re - z_re * a_im) / den
        lam_ref[0] = ab_re
        lam_ref[1] = ab_im
        for j in range(S5_PAIRS):
            fr = f_re[j:j + 1]
            fi = f_im[j:j + 1]
            br = bre_ref[j]
            bi = bim_ref[j]
            wb_ref[j, :, 0:LANES] = (fr * br - fi * bi).astype(BF16)
            wb_ref[j, :, LANES:2 * LANES] = (fr * bi + fi * br).astype(BF16)
            wc_ref[j, 0:LANES, :] = cre_ref[j].astype(BF16)
            wc_ref[j, LANES:2 * LANES, :] = (-cim_ref[j]).astype(BF16)
        hr_ref[...] = h0r_ref[...]
        hi_ref[...] = h0i_ref[...]

    u = u_ref[...].reshape(M, B_WIDTH)
    ub = u.astype(BF16)
    d = d_ref[...]
    per_quad = S5_PAIRS // S5_QUADS
    cols = [None] * S5_QUADS

    def b_proj(q):
        for j in range(q * per_quad, (q + 1) * per_quad):
            bu = _dot(ub[:, q * LANES:(q + 1) * LANES], wb_ref[j])
            sre_ref[j] = bu[:, 0:LANES]
            sim_ref[j] = bu[:, LANES:2 * LANES]

    def scan(q):
        pairs = range(q * per_quad, (q + 1) * per_quad)
        lr = [lam_ref[0, j:j + 1, :] for j in pairs]
        li = [lam_ref[1, j:j + 1, :] for j in pairs]
        hr = [hr_ref[:, j * LANES:(j + 1) * LANES] for j in pairs]
        hi = [hi_ref[:, j * LANES:(j + 1) * LANES] for j in pairs]
        for t in range(tT):
            rows = slice(t * B, (t + 1) * B)
            for k, j in enumerate(pairs):
                r = lr[k] * hr[k] - li[k] * hi[k] + sre_ref[j, rows, :]
                i = lr[k] * hi[k] + li[k] * hr[k] + sim_ref[j, rows, :]
                sre_ref[j, rows, :] = r
                sim_ref[j, rows, :] = i
                hr[k], hi[k] = r, i
        for k, j in enumerate(pairs):
            hr_ref[:, j * LANES:(j + 1) * LANES] = hr[k]
            hi_ref[:, j * LANES:(j + 1) * LANES] = hi[k]

    def c_proj(q):
        acc = None
        for j in range(q * per_quad, (q + 1) * per_quad):
            hs = jnp.concatenate([sre_ref[j], sim_ref[j]], axis=-1).astype(BF16)
            part = _dot(hs, wc_ref[j])
            acc = part if acc is None else acc + part
        cols[q] = acc + d[:, q * LANES:(q + 1) * LANES] * u[:, q * LANES:(q + 1) * LANES]

    b_proj(0)
    for q in range(S5_QUADS):
        if q + 1 < S5_QUADS:
            b_proj(q + 1)
        scan(q)
        if q >= 1:
            c_proj(q - 1)
    c_proj(S5_QUADS - 1)
    y = jnp.concatenate(cols, axis=-1)
    g = _gelu_tanh(y)
    out = g * jax.nn.sigmoid(_dot(g.astype(BF16), wglu_ref[...]))
    yb_ref[...] = out.reshape(tT, B, B_WIDTH)


def _s5(u_tm, prm, h0r, h0i, B, L, tT):
    M = tT * B
    state = S5_PAIRS * LANES
    kern = functools.partial(_s5_kernel, B=B, tT=tT)
    tile = lambda i: (i, 0, 0)
    return pl.pallas_call(
        kern,
        grid=(L // tT,),
        in_specs=[
            pl.BlockSpec((tT, B, B_WIDTH), tile),
            _const_spec((S5_PAIRS, LANES)),
            _const_spec((S5_PAIRS, LANES)),
            _const_spec((S5_PAIRS, LANES)),
            _const_spec((S5_PAIRS, LANES, LANES)),
            _const_spec((S5_PAIRS, LANES, LANES)),
            _const_spec((S5_PAIRS, LANES, LANES)),
            _const_spec((S5_PAIRS, LANES, LANES)),
            _const_spec((1, B_WIDTH)),
            _layer_spec(*prm["w_glu"]),
            _const_spec((B, state)),
            _const_spec((B, state)),
        ],
        out_specs=[
            pl.BlockSpec((tT, B, B_WIDTH), tile),
            pl.BlockSpec((B, state), lambda i: (0, 0)),
            pl.BlockSpec((B, state), lambda i: (0, 0)),
        ],
        out_shape=[
            jax.ShapeDtypeStruct((L, B, B_WIDTH), F32),
            jax.ShapeDtypeStruct((B, state), F32),
            jax.ShapeDtypeStruct((B, state), F32),
        ],
        scratch_shapes=[
            pltpu.VMEM((2, S5_PAIRS, LANES), F32),
            pltpu.VMEM((S5_PAIRS, LANES, 2 * LANES), BF16),
            pltpu.VMEM((S5_PAIRS, 2 * LANES, LANES), BF16),
            pltpu.VMEM((S5_PAIRS, M, LANES), F32),
            pltpu.VMEM((S5_PAIRS, M, LANES), F32),
        ],
        compiler_params=_params("arbitrary"),
        name="s5_scan",
    )(u_tm.reshape(L, B, B_WIDTH), prm["a_re"], prm["a_im"], prm["ldt"], prm["b_re"], prm["b_im"],
      prm["c_re"], prm["c_im"], prm["d"], prm["w_glu"][0], h0r, h0i)


def _odd_in_kernel(*refs, bB, tT, lc, kv_channel_major, n_alias):
    (h_ref, g_ref, w_ref, wkv_ref, wft_ref, bft_ref, gv_ref, ws_ref, bs_ref) = refs[:9]
    outs = refs[9 + n_alias:]
    if kv_channel_major:
        q_ref, k_ref, v_ref, kb_ref, vb_ref, lf_ref, ft_ref, yd_ref, carry_ref = outs
        vn_ref = None
    else:
        q_ref, k_ref, v_ref, kb_ref, vb_ref, lf_ref, ft_ref, yd_ref, vn_ref, carry_ref = outs
    M = bB * tT

    @pl.when(pl.program_id(1) == 0)
    def _():
        carry_ref[...] = jnp.zeros_like(carry_ref)

    x = h_ref[...].reshape(M, D_MODEL)
    hn = _rms(x, g_ref[...]).astype(BF16)
    z = _dot(hn, w_ref[...])
    q = z[:, 0:C_WIDTH] * (C_HEAD_DIM ** -0.5 * LOG2E)
    u = z[:, C_WIDTH:C_WIDTH + D_WIDTH]
    vd = z[:, C_WIDTH + D_WIDTH:C_WIDTH + 2 * D_WIDTH]
    q_ref[...] = q.reshape(bB, tT, C_WIDTH).astype(BF16)
    if kv_channel_major:
        kv = _dot_nt(wkv_ref[...], hn)
        k_ref[0, 0] = kv[0:C_WIDTH]
        v_ref[0, 0] = kv[C_WIDTH:]
        kb_ref[0] = kv[0:C_WIDTH].astype(BF16)
        ones = jnp.ones((C_HEAD_DIM, M), BF16)
        for hh in range(C_HEADS):
            v_at = hh * LANES + (hh % 2) * C_HEAD_DIM
            ones_at = hh * LANES + (1 - hh % 2) * C_HEAD_DIM
            vb_ref[0, v_at:v_at + C_HEAD_DIM, :] = (
                kv[C_WIDTH + hh * C_HEAD_DIM:C_WIDTH + (hh + 1) * C_HEAD_DIM].astype(BF16))
            vb_ref[0, ones_at:ones_at + C_HEAD_DIM, :] = ones
    else:
        k = z[:, C_WIDTH + 2 * D_WIDTH:2 * C_WIDTH + 2 * D_WIDTH].reshape(bB, tT, C_WIDTH)
        v = z[:, 2 * C_WIDTH + 2 * D_WIDTH:].reshape(bB, tT, C_WIDTH)
        k_ref[...] = k
        v_ref[...] = v
        kb_ref[...] = k.astype(BF16)
        vb_ref[...] = v.astype(BF16)

    logf_t = jax.nn.log_sigmoid(_dot_nt(wft_ref[...], hn) + bft_ref[...])
    ft = _cumsum_lanes(logf_t, tT) + carry_ref[...]
    for b in range(bB):
        lf_ref[b] = logf_t[:, b * tT:(b + 1) * tT]
        ft_ref[b] = ft[:, b * tT:(b + 1) * tT]
    if bB == 1:
        carry_ref[...] = ft[:, M - 1:M]

    vn = _rms(vd, gv_ref[...])
    if vn_ref is not None:
        vn_ref[...] = vn.reshape(bB, tT, D_WIDTH)
    vnb = vn.astype(BF16)
    rr = lax.broadcasted_iota(jnp.int32, (lc, lc), 0)
    cc = lax.broadcasted_iota(jnp.int32, (lc, lc), 1)
    wsm = [jnp.where(cc <= rr, ws_ref[hh], 0.0).astype(BF16) for hh in range(C_HEADS)]
    first_head = lax.broadcasted_iota(jnp.int32, (lc, LANES), 1) < C_HEAD_DIM
    bias = bs_ref[...]
    for ch in range(M // lc):
        rows = slice(ch * lc, (ch + 1) * lc)
        cols = []
        for hp in range(HEAD_PAIRS):
            vp = vnb[rows, hp * LANES:(hp + 1) * LANES]
            cols.append(jnp.where(first_head, _dot(wsm[2 * hp], vp), _dot(wsm[2 * hp + 1], vp)))
        mixed = jnp.concatenate(cols, axis=-1) + bias
        b, off = divmod(ch * lc, tT)
        yd_ref[b, off:off + lc, :] = (u[rows] * mixed).astype(BF16)


def _odd_in(h, lw, bB, tT, j, kv_stacks):
    B, L, _ = h.shape
    nT = L // tT
    assert bB == 1 or nT == 1
    kv_channel_major = kv_stacks is not None
    lc = lw["w_s"].shape[-1]
    tile = lambda b, t: (b, t, 0)
    chan = lambda b, t: (b, 0, t)

    def act(width, dtype):
        return pl.BlockSpec((bB, tT, width), tile), jax.ShapeDtypeStruct((B, L, width), dtype)

    heads = (pl.BlockSpec((bB, C_HEADS, tT), chan), jax.ShapeDtypeStruct((B, C_HEADS, L), F32))
    if kv_channel_major:
        assert bB == 1
        stack = (pl.BlockSpec((1, 1, C_WIDTH, tT), lambda b, t: (j, b, 0, t)),
                 jax.ShapeDtypeStruct((N_ODD, B, C_WIDTH, L), F32))
        chan_bf = (pl.BlockSpec((1, C_WIDTH, tT), chan), jax.ShapeDtypeStruct((B, C_WIDTH, L), BF16))
        chan_v = (pl.BlockSpec((1, C_HEADS * LANES, tT), chan), jax.ShapeDtypeStruct((B, C_HEADS * LANES, L), BF16))
        outs = [act(C_WIDTH, BF16), stack, stack, chan_bf, chan_v, heads, heads, act(D_WIDTH, BF16)]
        w_kv_spec = _const_spec((2 * C_WIDTH, D_MODEL))
    else:
        outs = [act(C_WIDTH, BF16), act(C_WIDTH, F32), act(C_WIDTH, F32), act(C_WIDTH, BF16), act(C_WIDTH, BF16),
                heads, heads, act(D_WIDTH, BF16), act(D_WIDTH, F32)]
        w_kv_spec = _const_spec((1, 1))
    alias_in = list(kv_stacks) if kv_channel_major else []
    n_fixed = 9
    kern = functools.partial(_odd_in_kernel, bB=bB, tT=tT, lc=lc, kv_channel_major=kv_channel_major,
                             n_alias=len(alias_in))
    w_in = lw["w_in_cm"] if kv_channel_major else lw["w_in_rm"]
    w_kv = lw["w_kv_t"] if kv_channel_major else jnp.zeros((1, 1), BF16)
    return pl.pallas_call(
        kern,
        grid=(B // bB, nT),
        in_specs=[
            pl.BlockSpec((bB, tT, D_MODEL), tile),
            _const_spec((1, D_MODEL)),
            _const_spec(w_in.shape),
            w_kv_spec,
            _const_spec((C_HEADS, D_MODEL)),
            _const_spec((C_HEADS, 1)),
            _const_spec((1, D_WIDTH)),
            _const_spec((C_HEADS, lc, lc)),
            _const_spec((lc, D_WIDTH)),
        ] + [pl.BlockSpec(memory_space=pl.ANY)] * len(alias_in),
        out_specs=[o[0] for o in outs],
        out_shape=[o[1] for o in outs],
        input_output_aliases={n_fixed + i: 1 + i for i in range(len(alias_in))},
        scratch_shapes=[pltpu.VMEM((C_HEADS, 1), F32)],
        compiler_params=_params("arbitrary", "arbitrary"),
        name="odd_in",
    )(h, lw["g_pre"], w_in, w_kv, lw["w_ft"], lw["b_ft"], lw["g_v"], lw["w_s"], lw["b_s"], *alias_in)


def _attn_prompt_kernel(qi_ref, kj_ref, q_ref, kt_ref, vt_ref, ft_ref, o_ref, m_ref, acc_ref, *, tq):
    p = pl.program_id(1)
    qi = qi_ref[p]
    kj = kj_ref[p]
    first_head = lax.broadcasted_iota(jnp.int32, (tq, LANES), 1) < C_HEAD_DIM

    @pl.when(kj == 0)
    def _():
        m_ref[...] = jnp.full_like(m_ref, -jnp.inf)
        acc_ref[...] = jnp.zeros_like(acc_ref)

    def block(diagonal):
        if diagonal:
            visible = (lax.broadcasted_iota(jnp.int32, (tq, tq), 1)
                       <= lax.broadcasted_iota(jnp.int32, (tq, tq), 0))
        fk = ft_ref[0] * LOG2E
        for hp in range(HEAD_PAIRS):
            lanes = slice(hp * LANES, (hp + 1) * LANES)
            qp = q_ref[0, :, lanes]
            kt = kt_ref[0, lanes, :]
            qs = (jnp.where(first_head, qp, jnp.zeros_like(qp)), jnp.where(first_head, jnp.zeros_like(qp), qp))
            done = []
            for e in range(2):
                h = 2 * hp + e
                s = _dot(qs[e], kt) - fk[h:h + 1]
                if diagonal:
                    s = jnp.where(visible, s, MASK_VALUE)
                m_old = m_ref[h]
                m_new = jnp.maximum(m_old, jnp.max(s, axis=-1, keepdims=True))
                a = jnp.exp2(m_old - m_new)
                pr = jnp.exp2(s - jnp.concatenate([m_new] * (tq // LANES), axis=-1))
                acc = acc_ref[h] * a + _dot_nt(pr.astype(BF16), vt_ref[0, h * LANES:(h + 1) * LANES, :])
                if diagonal:
                    done.append(acc)
                else:
                    m_ref[h] = m_new
                    acc_ref[h] = acc
            if diagonal:
                num = jnp.where(first_head, done[0], done[1])
                den = pltpu.roll(jnp.where(first_head, done[1], done[0]), C_HEAD_DIM, axis=1)
                o_ref[0, :, lanes] = (num / den).astype(BF16)

    @pl.when(kj < qi)
    def _():
        block(False)

    @pl.when(kj == qi)
    def _():
        block(True)


def _attn_prompt(q, ktb, vtb, ft, tq):
    B, L, _ = q.shape
    nq = L // tq
    pairs = [(i, k) for i in range(nq) for k in range(i + 1)]
    qi = jnp.asarray(np.array([p[0] for p in pairs], np.int32))
    kj = jnp.asarray(np.array([p[1] for p in pairs], np.int32))
    kern = functools.partial(_attn_prompt_kernel, tq=tq)
    q_spec = pl.BlockSpec((1, tq, C_WIDTH), lambda b, p, qi, kj: (b, qi[p], 0))
    return pl.pallas_call(
        kern,
        grid_spec=pltpu.PrefetchScalarGridSpec(
            num_scalar_prefetch=2,
            grid=(B, len(pairs)),
            in_specs=[q_spec,
                      pl.BlockSpec((1, C_WIDTH, tq), lambda b, p, qi, kj: (b, 0, kj[p])),
                      pl.BlockSpec((1, C_HEADS * LANES, tq), lambda b, p, qi, kj: (b, 0, kj[p])),
                      pl.BlockSpec((1, C_HEADS, tq), lambda b, p, qi, kj: (b, 0, kj[p]))],
            out_specs=q_spec,
            scratch_shapes=[pltpu.VMEM((C_HEADS, tq, LANES), F32), pltpu.VMEM((C_HEADS, tq, LANES), F32)],
        ),
        out_shape=jax.ShapeDtypeStruct((B, L, C_WIDTH), BF16),
        compiler_params=_params("arbitrary", "arbitrary"),
        name="attn_prompt",
    )(qi, kj, q, ktb, vtb, ft)


def _attn_sample_kernel(q_ref, kn_ref, vn_ref, kpt_ref, vpt_ref, lpt_ref, ft_ref, o_ref, *, L, P):
    first_head = lax.broadcasted_iota(jnp.int32, (L, LANES), 1) < C_HEAD_DIM
    past_cum = _cumsum_lanes(lpt_ref[0, 0], P)
    past_after = (past_cum[:, P - 1:P] - past_cum) * LOG2E
    causal = lax.broadcasted_iota(jnp.int32, (L, L), 1) <= lax.broadcasted_iota(jnp.int32, (L, L), 0)
    ft = ft_ref[0] * LOG2E
    for hp in range(HEAD_PAIRS):
        lanes = slice(hp * LANES, (hp + 1) * LANES)
        qp = q_ref[0, :, lanes]
        qs = (jnp.where(first_head, qp, jnp.zeros_like(qp)), jnp.where(first_head, jnp.zeros_like(qp), qp))
        kpt = kpt_ref[0, 0, lanes, :].astype(BF16)
        vpt = vpt_ref[0, 0, lanes, :].astype(BF16)
        kn = kn_ref[0, :, lanes]
        vn = vn_ref[0, :, lanes]
        outs = []
        for e in range(2):
            h = 2 * hp + e
            sp = _dot(qs[e], kpt) + past_after[h:h + 1]
            sn = jnp.where(causal, _dot_nt(qs[e], kn) - ft[h:h + 1], MASK_VALUE)
            m = jnp.maximum(jnp.max(sp, axis=-1, keepdims=True), jnp.max(sn, axis=-1, keepdims=True))
            pp = jnp.exp2(sp - m)
            pn = jnp.exp2(sn - m)
            den = jnp.sum(pp, axis=-1, keepdims=True) + jnp.sum(pn, axis=-1, keepdims=True)
            outs.append((_dot_nt(pp.astype(BF16), vpt) + _dot(pn.astype(BF16), vn)) / den)
        o_ref[0, :, lanes] = jnp.where(first_head, outs[0], outs[1]).astype(BF16)


def _attn_sample(q, kb, vb, k_past_t, v_past_t, logf_past_t, ft, j):
    B, L, _ = q.shape
    P = k_past_t.shape[-1]
    kern = functools.partial(_attn_sample_kernel, L=L, P=P)
    new = pl.BlockSpec((1, L, C_WIDTH), lambda b: (b, 0, 0))
    past = pl.BlockSpec((1, 1, C_WIDTH, P), lambda b: (j, b, 0, 0))
    return pl.pallas_call(
        kern,
        grid=(B,),
        in_specs=[new, new, new, past, past,
                  pl.BlockSpec((1, 1, C_HEADS, P), lambda b: (j, b, 0, 0)),
                  pl.BlockSpec((1, C_HEADS, L), lambda b: (b, 0, 0))],
        out_specs=new,
        out_shape=jax.ShapeDtypeStruct((B, L, C_WIDTH), BF16),
        compiler_params=_params("arbitrary"),
        name="attn_sample",
    )(q, kb, vb, k_past_t, v_past_t, logf_past_t, ft)


def _ffn_kernel(h_ref, a_ref, b_ref, wo_ref, gpost_ref, gpre_ref, wup_ref, wcv_ref, fh_ref, wdn_ref, gfp_ref,
                wgate_ref, pe_ref, wple_ref, o_ref, nf_ref, cb_ref, act_ref, *, bB, tT, b_time_major):
    M = bB * tT

    @pl.when(pl.program_id(1) == 0)
    def _():
        nf_ref[...] = fh_ref[0]

    a = a_ref[...].reshape(M, a_ref.shape[-1])
    if b_time_major:
        width = b_ref.shape[-1] // bB
        bpart = jnp.concatenate([b_ref[:, i * width:(i + 1) * width] for i in range(bB)], axis=0)
    else:
        bpart = b_ref[...].reshape(M, b_ref.shape[-1])
    y = _dot(jnp.concatenate([a.astype(BF16), bpart.astype(BF16)], axis=-1), wo_ref[...])
    h1 = h_ref[...].reshape(M, D_MODEL) + _rms(y, gpost_ref[...])
    hn = _rms(h1, gpre_ref[...]).astype(BF16)

    def up_proj(c):
        return [_dot(hn, wup_ref[:, col0:col0 + FF_CHUNK]).reshape(bB, tT, FF_CHUNK)
                for col0 in (c * FF_CHUNK, D_FF + c * FF_CHUNK)]

    ups = up_proj(0)
    for c in range(N_FF_CHUNKS):
        nxt = up_proj(c + 1) if c + 1 < N_FF_CHUNKS else None
        cb = cb_ref.at[c % 2]
        halves = []
        for half, col0 in enumerate((c * FF_CHUNK, D_FF + c * FF_CHUNK)):
            cols = slice(col0, col0 + FF_CHUNK)
            uc, last = _causal_conv3(cb.at[:, :, half * FF_CHUNK:(half + 1) * FF_CHUNK],
                                     ups[half], nf_ref[:, :, cols], wcv_ref[:, cols], tT)
            nf_ref[:, :, cols] = last
            halves.append(uc)
        act_ref[:, c * FF_CHUNK:(c + 1) * FF_CHUNK] = (
            (_gelu_tanh(halves[0]) * halves[1]).reshape(M, FF_CHUNK).astype(BF16))
        ups = nxt

    h2 = h1 + _rms(_dot(act_ref[...], wdn_ref[...]), gfp_ref[...])
    gate = jax.nn.sigmoid(_dot(h2.astype(BF16), wgate_ref[...]))
    pe = pe_ref[0].reshape(M, PLE_DIM).astype(BF16)
    h3 = h2 + gate * _dot(pe, wple_ref[...])
    o_ref[...] = h3.reshape(bB, tT, D_MODEL)


def _ffn(h, a, b, b_time_major, lw, fh, pe, layer, bB, tT):
    B, L, _ = h.shape
    grid = (B // bB, L // tT)
    kern = functools.partial(_ffn_kernel, bB=bB, tT=tT, b_time_major=b_time_major)
    tile = lambda bi, t: (bi, t, 0)
    wa = a.shape[-1]
    if b_time_major:
        wb = b.shape[-1] // B
        b_spec = pl.BlockSpec((tT, bB * wb), lambda bi, t: (t, bi))
    else:
        wb = b.shape[-1]
        b_spec = pl.BlockSpec((bB, tT, wb), tile)
    fslot = layer if fh.shape[0] > 1 else 0
    return pl.pallas_call(
        kern,
        grid=grid,
        in_specs=[
            pl.BlockSpec((bB, tT, D_MODEL), tile),
            pl.BlockSpec((bB, tT, wa), tile),
            b_spec,
            _layer_spec(*lw["w_out"]),
            _const_spec((1, D_MODEL)),
            _const_spec((1, D_MODEL)),
            _layer_spec(*lw["w_up"]),
            _const_spec((FFN_CONV, 2 * D_FF)),
            pl.BlockSpec((1, bB, HIST, 2 * D_FF), lambda bi, t: (fslot, bi, 0, 0)),
            _layer_spec(*lw["w_down"]),
            _const_spec((1, D_MODEL)),
            _layer_spec(*lw["w_gate"]),
            pl.BlockSpec((1, bB, tT, PLE_DIM), lambda bi, t: (layer, bi, t, 0)),
            _layer_spec(*lw["w_ple"]),
        ],
        out_specs=[pl.BlockSpec((bB, tT, D_MODEL), tile),
                   pl.BlockSpec((bB, HIST, 2 * D_FF), lambda bi, t: (bi, 0, 0))],
        out_shape=[
            jax.ShapeDtypeStruct((B, L, D_MODEL), F32),
            jax.ShapeDtypeStruct((B, HIST, 2 * D_FF), F32),
        ],
        scratch_shapes=[
            pltpu.VMEM((2, bB, tT + CONV_PAD, 2 * FF_CHUNK), F32),
            pltpu.VMEM((bB * tT, D_FF), BF16),
        ],
        compiler_params=_params("arbitrary", "arbitrary"),
        name="mix_out_ffn",
    )(h, a, b, lw["w_out"][0], lw["g_post"], lw["g_ffn_pre"], lw["w_up"][0], lw["w_fconv"], fh, lw["w_down"][0],
      lw["g_ffn_post"], lw["w_gate"][0], pe, lw["w_ple"][0])


def _s5_block_layout(x, state_major):
    per_quad = S5_PAIRS // S5_QUADS
    eye2 = jnp.eye(2, dtype=x.dtype)
    slot = jax.nn.one_hot(jnp.arange(S5_PAIRS) % per_quad, per_quad, dtype=x.dtype)
    if state_major:
        x4 = x.reshape(S5_PAIRS, 2, S5_STATE, S5_GROUP)
        blk = jnp.einsum("jgpc,gh->jgchp", x4, eye2).reshape(S5_PAIRS, 2 * S5_GROUP, LANES)
        return jnp.einsum("jkn,jr->jrkn", blk, slot).reshape(S5_PAIRS, LANES, LANES)
    x4 = x.reshape(S5_PAIRS, 2, S5_GROUP, S5_STATE)
    blk = jnp.einsum("jgcp,gh->jgphc", x4, eye2).reshape(S5_PAIRS, LANES, 2 * S5_GROUP)
    return jnp.einsum("jnk,jr->jnrk", blk, slot).reshape(S5_PAIRS, LANES, LANES)


STACKED_MATMUL_WEIGHTS = ("w_ffn_up", "w_ffn_down", "w_ple_gate", "w_ple", "w_even_in", "w_even_out", "w_glu",
                          "w_odd_out")


def _layer_params(i, W, WB):
    j = i // 2
    lw = {
        "g_pre": W["g_mix_pre"][i][None], "g_post": W["g_mix_post"][i][None],
        "g_ffn_pre": W["g_ffn_pre"][i][None], "g_ffn_post": W["g_ffn_post"][i][None],
        "w_up": (WB["w_ffn_up"], i), "w_fconv": W["w_ffn_conv"][i], "w_down": (WB["w_ffn_down"], i),
        "w_gate": (WB["w_ple_gate"], i), "w_ple": (WB["w_ple"], i),
    }
    if i % 2 == 0:
        lw.update({
            "w_in": (WB["w_even_in"], j), "w_conv": W["w_conv_a"][j], "w_out": (WB["w_even_out"], j),
            "s5": {
                "a_re": W["s5_a_re"][j].reshape(S5_PAIRS, LANES),
                "a_im": W["s5_a_im"][j].reshape(S5_PAIRS, LANES),
                "ldt": jnp.broadcast_to(W["s5_log_dt"][j][:, None], (S5_GROUPS, S5_STATE)).reshape(S5_PAIRS, LANES),
                "b_re": _s5_block_layout(W["s5_b_re"][j], True), "b_im": _s5_block_layout(W["s5_b_im"][j], True),
                "c_re": _s5_block_layout(W["s5_c_re"][j], False), "c_im": _s5_block_layout(W["s5_c_im"][j], False),
                "d": W["s5_d"][j].reshape(1, B_WIDTH), "w_glu": (WB["w_glu"], j),
            },
        })
    else:
        w = W["w_odd_in"][j].astype(BF16)
        f0 = 3 * C_WIDTH
        w_q, w_kv, w_f, w_ud = w[:, :C_WIDTH], w[:, C_WIDTH:f0], w[:, f0:f0 + C_HEADS], w[:, f0 + C_HEADS:]
        lw.update({
            "w_in_cm": jnp.concatenate([w_q, w_ud], axis=1),
            "w_in_rm": jnp.concatenate([w_q, w_ud, w_kv], axis=1),
            "w_kv_t": w_kv.T, "w_ft": w_f.T, "b_ft": W["b_forget"][j][:, None],
            "g_v": W["g_gmlp_v"][j][None],
            "w_out": (WB["w_odd_out"], j),
        })
        for lc in (GMLP_CHUNK, 32):
            lw[("w_s", lc)] = W["w_spatial"][j][:, :lc, :lc]
            lw[("b_s", lc)] = jnp.repeat(W["b_spatial"][j][:, :lc].T, D_WIDTH // C_HEADS, axis=1)
    return lw


def _run_trunk(x, pe, st, LW, bB, tT, tT_s5, tq, tT_ffn):
    B, L, _ = x.shape
    prompt = st is None
    lc = min(L, GMLP_CHUNK)
    h = x
    conv_a, ssm_re, ssm_im, ks, vs, lfs, gvs, ffs = [], [], [], [], [], [], [], []
    kv_stacks = ()
    fh = jnp.zeros((1, B, HIST, 2 * D_FF), F32) if prompt else st["ffn"]
    for i in range(DEPTH):
        j = i // 2
        lw = LW[i]
        if i % 2 == 0:
            if prompt:
                hist = jnp.zeros((B, HIST, A_WIDTH), F32)
                h0r = jnp.zeros((B, S5_GROUPS * S5_STATE), F32)
                h0i = h0r
            else:
                hist = st["conv_a"][j]
                h0r = st["ssm_re"][j].reshape(B, S5_GROUPS * S5_STATE)
                h0i = st["ssm_im"][j].reshape(B, S5_GROUPS * S5_STATE)
            ya, u_tm, new_hist = _even_in(h, lw["g_pre"], lw["w_in"], lw["w_conv"], hist, bB, tT)
            yb_tm, nr, ni = _s5(u_tm, lw["s5"], h0r, h0i, B, L, tT_s5)
            a, b, b_tm = ya, yb_tm.reshape(L, B * B_WIDTH), True
            conv_a.append(new_hist)
            ssm_re.append(nr.reshape(B, S5_GROUPS, S5_STATE))
            ssm_im.append(ni.reshape(B, S5_GROUPS, S5_STATE))
        else:
            lw = dict(lw, w_s=lw[("w_s", lc)], b_s=lw[("b_s", lc)])
            if prompt:
                q, k_stack, v_stack, ktb, vtb, logf_t, ft, yd = _odd_in(h, lw, bB, tT, j, kv_stacks)
                kv_stacks = (k_stack, v_stack)
                att = _attn_prompt(q, ktb, vtb, ft, tq)
            else:
                q, k, v, kb, vb, logf_t, ft, yd, vn = _odd_in(h, lw, bB, tT, j, None)
                att = _attn_sample(q, kb, vb, st["k_t"], st["v_t"], st["logf_t"], ft, j)
                ks.append(k.reshape(B, L, C_HEADS, C_HEAD_DIM))
                vs.append(v.reshape(B, L, C_HEADS, C_HEAD_DIM))
                gvs.append(vn)
            a, b, b_tm = att, yd, False
            lfs.append(logf_t)
        h, new_f = _ffn(h, a, b, b_tm, lw, fh, pe, i, bB, tT_ffn)
        ffs.append(new_f)
    new_state = {"conv_a": jnp.stack(conv_a), "ssm_re": jnp.stack(ssm_re), "ssm_im": jnp.stack(ssm_im),
                 "logf": jnp.swapaxes(jnp.stack(lfs), 2, 3), "ffn": jnp.stack(ffs)}
    if prompt:
        for name, stack in zip(("k", "v"), kv_stacks):
            new_state[name] = jnp.transpose(stack.reshape(N_ODD, B, C_HEADS, C_HEAD_DIM, L), (0, 1, 4, 2, 3))
    else:
        new_state.update({"k": jnp.stack(ks), "v": jnp.stack(vs), "gmlp_v": jnp.stack(gvs)})
    return h, new_state


def kernel(x_prompt, x_sample, p_prompt, p_sample, cache_conv_a, state_ssm_re, state_ssm_im, cache_k, cache_v, cache_logf, cache_ffn_conv, g_mix_pre, g_mix_post, g_ffn_pre, g_ffn_post, w_even_in, w_conv_a, s5_a_re, s5_a_im, s5_log_dt, s5_b_re, s5_b_im, s5_c_re, s5_c_im, s5_d, w_glu, w_even_out, w_odd_in, b_forget, w_spatial, b_spatial, g_gmlp_v, w_odd_out, w_ffn_up, w_ffn_conv, w_ffn_down, w_ple, w_ple_gate):
    W = {"g_mix_pre": g_mix_pre, "g_mix_post": g_mix_post, "g_ffn_pre": g_ffn_pre, "g_ffn_post": g_ffn_post,
         "w_even_in": w_even_in, "w_conv_a": w_conv_a, "s5_a_re": s5_a_re, "s5_a_im": s5_a_im,
         "s5_log_dt": s5_log_dt, "s5_b_re": s5_b_re, "s5_b_im": s5_b_im, "s5_c_re": s5_c_re,
         "s5_c_im": s5_c_im, "s5_d": s5_d, "w_glu": w_glu, "w_even_out": w_even_out,
         "w_odd_in": w_odd_in, "b_forget": b_forget, "w_spatial": w_spatial, "b_spatial": b_spatial,
         "g_gmlp_v": g_gmlp_v, "w_odd_out": w_odd_out, "w_ffn_up": w_ffn_up, "w_ffn_conv": w_ffn_conv,
         "w_ffn_down": w_ffn_down, "w_ple": w_ple, "w_ple_gate": w_ple_gate}
    WB = {name: W[name].astype(BF16) for name in STACKED_MATMUL_WEIGHTS}
    LW = [_layer_params(i, W, WB) for i in range(DEPTH)]
    y_prompt, sp = _run_trunk(x_prompt, p_prompt, None, LW, bB=1, tT=1024, tT_s5=64, tq=512, tT_ffn=512)
    dec_b, dec_l, _ = x_sample.shape
    past = cache_k.shape[2]
    st = {"conv_a": cache_conv_a, "ssm_re": state_ssm_re, "ssm_im": state_ssm_im, "ffn": cache_ffn_conv,
          "k_t": jnp.transpose(cache_k, (0, 1, 3, 4, 2)).reshape(N_ODD, dec_b, C_WIDTH, past),
          "v_t": jnp.transpose(cache_v, (0, 1, 3, 4, 2)).reshape(N_ODD, dec_b, C_WIDTH, past),
          "logf_t": jnp.swapaxes(cache_logf, 2, 3)}
    y_sample, ss = _run_trunk(x_sample, p_sample, st, LW, bB=dec_b, tT=dec_l, tT_s5=dec_l, tq=dec_l, tT_ffn=dec_l)
    return (y_prompt, y_sample,
            sp["conv_a"], sp["ssm_re"], sp["ssm_im"], sp["k"], sp["v"], sp["logf"], sp["ffn"],
            ss["conv_a"], ss["ssm_re"], ss["ssm_im"], ss["k"], ss["v"], ss["logf"], ss["gmlp_v"], ss["ffn"])
```

```python
import functools

import jax
import jax.numpy as jnp
import numpy as np
from jax import lax
from jax.experimental import pallas as pl
from jax.experimental.pallas import tpu as pltpu

F32 = jnp.float32
BF16 = jnp.bfloat16

D_MODEL = 1024
DEPTH = 4
N_ODD = DEPTH // 2
PLE_DIM = 256
RMS_EPS = 1e-6
A_WIDTH = 512
A_CONV = 3
B_WIDTH = 512
S5_GROUP = 16
S5_GROUPS = 32
S5_STATE = 64
S5_PAIRS = S5_GROUPS // 2
S5_QUADS = B_WIDTH // 128
C_HEADS = 8
C_HEAD_DIM = 64
C_WIDTH = 512
HEAD_PAIRS = C_HEADS // 2
D_WIDTH = 512
GMLP_CHUNK = 128
D_FF = 2816
FFN_CONV = 3
FF_CHUNK = 256
N_FF_CHUNKS = D_FF // FF_CHUNK
MASK_VALUE = -1e30
CONV_PAD = 8
HIST = A_CONV - 1

V7X_VMEM_LIMIT = 56 * 1024 * 1024
LANES = 128
LOG2E = 1.4426950408889634


def _rms(x, g):
    return x * lax.rsqrt(jnp.mean(x * x, axis=-1, keepdims=True) + RMS_EPS) * g


GELU_A = 0.7978845608028654
GELU_B = GELU_A * 0.044715


def _gelu_tanh(x):
    half = 0.5 * x
    return half + half * jnp.tanh(x * (GELU_A + GELU_B * (x * x)))


def _dot(a, b):
    return jnp.dot(a, b, preferred_element_type=F32)


def _dot_nt(a, b):
    return lax.dot_general(a, b, (((1,), (1,)), ((), ())), preferred_element_type=F32)


def _cumsum_lanes(x, seg):
    pos = lax.broadcasted_iota(jnp.int32, x.shape, 1)
    if seg != x.shape[-1]:
        pos = pos % seg
    shift = 1
    while shift < seg:
        x = x + jnp.where(pos >= shift, pltpu.roll(x, shift, axis=1), 0.0)
        shift *= 2
    return x


def _const_spec(shape):
    zeros = (0,) * len(shape)
    return pl.BlockSpec(shape, lambda *_: zeros, pipeline_mode=pl.Buffered(1))


def _layer_spec(stack, idx):
    shape = stack.shape[1:]
    zeros = (0,) * len(shape)
    return pl.BlockSpec((None,) + shape, lambda *_: (idx,) + zeros, pipeline_mode=pl.Buffered(1))


def _params(*sem):
    return pltpu.CompilerParams(dimension_semantics=sem, vmem_limit_bytes=V7X_VMEM_LIMIT)


def _causal_conv3(cb_ref, cur, hist, w, tT):
    if cur.shape[0] == 1:
        x = cur[0]
        h = hist[0]
        row = lax.broadcasted_iota(jnp.int32, (8, x.shape[-1]), 0)
        s1 = pltpu.roll(x, 1, axis=0)
        s2 = pltpu.roll(x, 2, axis=0)
        head1 = jnp.where(row == 0, h[1:2], s1[0:8])
        head2 = jnp.where(row == 0, h[0:1], jnp.where(row == 1, h[1:2], s2[0:8]))
        s1 = jnp.concatenate([head1, s1[8:]], axis=0)
        s2 = jnp.concatenate([head2, s2[8:]], axis=0)
        y = w[0:1] * s2 + w[1:2] * s1 + w[2:3] * x
        return y[None], x[tT - HIST:tT][None]
    cb_ref[:, CONV_PAD - HIST:CONV_PAD, :] = hist
    cb_ref[:, CONV_PAD:CONV_PAD + tT, :] = cur
    y = w[0:1][None] * cb_ref[:, CONV_PAD - 2:CONV_PAD - 2 + tT, :]
    y = y + w[1:2][None] * cb_ref[:, CONV_PAD - 1:CONV_PAD - 1 + tT, :]
    y = y + w[2:3][None] * cur
    return y, cb_ref[:, CONV_PAD + tT - HIST:CONV_PAD + tT, :]


def _even_in_kernel(h_ref, g_ref, w_ref, wc_ref, hist_ref, ya_ref, u_ref, nh_ref, cb_ref, *, bB, tT):
    @pl.when(pl.program_id(1) == 0)
    def _():
        nh_ref[...] = hist_ref[...]

    x = h_ref[...].reshape(bB * tT, D_MODEL)
    hn = _rms(x, g_ref[...])
    z = _dot(hn.astype(BF16), w_ref[...])
    gb = z[:, 0:A_WIDTH]
    gc = z[:, A_WIDTH:2 * A_WIDTH]
    xa = z[:, 2 * A_WIDTH:3 * A_WIDTH]
    u = z[:, 3 * A_WIDTH:]
    cx = (gc * xa).reshape(bB, tT, A_WIDTH)
    yc, last = _causal_conv3(cb_ref, cx, nh_ref[...], wc_ref[...], tT)
    nh_ref[...] = last
    ya_ref[...] = (gb.reshape(bB, tT, A_WIDTH) * yc).astype(BF16)
    for b in range(bB):
        u_ref[:, b * B_WIDTH:(b + 1) * B_WIDTH] = u[b * tT:(b + 1) * tT]


def _even_in(h, g, w, wc, hist, bB, tT):
    B, L, _ = h.shape
    grid = (B // bB, L // tT)
    kern = functools.partial(_even_in_kernel, bB=bB, tT=tT)
    return pl.pallas_call(
        kern,
        grid=grid,
        in_specs=[
            pl.BlockSpec((bB, tT, D_MODEL), lambda b, t: (b, t, 0)),
            _layer_spec(*g),
            _layer_spec(*w),
            _layer_spec(*wc),
            pl.BlockSpec((bB, HIST, A_WIDTH), lambda b, t: (b, 0, 0)),
        ],
        out_specs=[
            pl.BlockSpec((bB, tT, A_WIDTH), lambda b, t: (b, t, 0)),
            pl.BlockSpec((tT, bB * B_WIDTH), lambda b, t: (t, b)),
            pl.BlockSpec((bB, HIST, A_WIDTH), lambda b, t: (b, 0, 0)),
        ],
        out_shape=[
            jax.ShapeDtypeStruct((B, L, A_WIDTH), BF16),
            jax.ShapeDtypeStruct((L, B * B_WIDTH), F32),
            jax.ShapeDtypeStruct((B, HIST, A_WIDTH), F32),
        ],
        scratch_shapes=[pltpu.VMEM((bB, tT + CONV_PAD, A_WIDTH), F32)],
        compiler_params=_params("arbitrary", "arbitrary"),
        name="even_in",
    )(h, g[0], w[0], wc[0], hist)


def _s5_kernel(u_ref, are_ref, aim_ref, ldt_ref, bre_ref, bim_ref, cre_ref, cim_ref, d_ref, wglu_ref,
               h0r_ref, h0i_ref, yb_ref, hr_ref, hi_ref, lam_ref, wb_ref, wc_ref, sre_ref, sim_ref, *, B, tT):
    M = tT * B

    @pl.when(pl.program_id(0) == 0)
    def _():
        a_re = are_ref[...]
        a_im = aim_ref[...]
        dt = jnp.exp(ldt_ref[...])
        mag = jnp.exp(a_re * dt)
        ab_re = mag * jnp.cos(a_im * dt)
        ab_im = mag * jnp.sin(a_im * dt)
        z_re = ab_re - 1.0
        den = a_re * a_re + a_im * a_im
        f_re = (z_re * a_re + ab_im * a_im) / den
        f_im = (ab_im * a_re - z_re * a_im) / den
        lam_ref[0] = ab_re
        lam_ref[1] = ab_im
        for j in range(S5_PAIRS):
            fr = f_re[j:j + 1]
            fi = f_im[j:j + 1]
            br = bre_ref[j]
            bi = bim_ref[j]
            wb_ref[j, :, 0:LANES] = (fr * br - fi * bi).astype(BF16)
            wb_ref[j, :, LANES:2 * LANES] = (fr * bi + fi * br).astype(BF16)
            wc_ref[j, 0:LANES, :] = cre_ref[j].astype(BF16)
            wc_ref[j, LANES:2 * LANES, :] = (-cim_ref[j]).astype(BF16)
        hr_ref[...] = h0r_ref[...]
        hi_ref[...] = h0i_ref[...]

    u = u_ref[...].reshape(M, B_WIDTH)
    ub = u.astype(BF16)
    d = d_ref[...]
    per_quad = S5_PAIRS // S5_QUADS
    cols = [None] * S5_QUADS

    def b_proj(q):
        for j in range(q * per_quad, (q + 1) * per_quad):
            bu = _dot(ub[:, q * LANES:(q + 1) * LANES], wb_ref[j])
            sre_ref[j] = bu[:, 0:LANES]
            sim_ref[j] = bu[:, LANES:2 * LANES]

    def scan(q):
        pairs = range(q * per_quad, (q + 1) * per_quad)
        lr = [lam_ref[0, j:j + 1, :] for j in pairs]
        li = [lam_ref[1, j:j + 1, :] for j in pairs]
        hr = [hr_ref[:, j * LANES:(j + 1) * LANES] for j in pairs]
        hi = [hi_ref[:, j * LANES:(j + 1) * LANES] for j in pairs]
        for t in range(tT):
            rows = slice(t * B, (t + 1) * B)
            for k, j in enumerate(pairs):
                r = lr[k] * hr[k] - li[k] * hi[k] + sre_ref[j, rows, :]
                i = lr[k] * hi[k] + li[k] * hr[k] + sim_ref[j, rows, :]
                sre_ref[j, rows, :] = r
                sim_ref[j, rows, :] = i
                hr[k], hi[k] = r, i
        for k, j in enumerate(pairs):
            hr_ref[:, j * LANES:(j + 1) * LANES] = hr[k]
            hi_ref[:, j * LANES:(j + 1) * LANES] = hi[k]

    def c_proj(q):
        acc = None
        for j in range(q * per_quad, (q + 1) * per_quad):
            hs = jnp.concatenate([sre_ref[j], sim_ref[j]], axis=-1).astype(BF16)
            part = _dot(hs, wc_ref[j])
            acc = part if acc is None else acc + part
        cols[q] = acc + d[:, q * LANES:(q + 1) * LANES] * u[:, q * LANES:(q + 1) * LANES]

    b_proj(0)
    for q in range(S5_QUADS):
        if q + 1 < S5_QUADS:
            b_proj(q + 1)
        scan(q)
        if q >= 1:
            c_proj(q - 1)
    c_proj(S5_QUADS - 1)
    y = jnp.concatenate(cols, axis=-1)
    g = _gelu_tanh(y)
    out = g * jax.nn.sigmoid(_dot(g.astype(BF16), wglu_ref[...]))
    yb_ref[...] = out.reshape(tT, B, B_WIDTH).astype(yb_ref.dtype)


def _s5(u_tm, prm, h0r, h0i, B, L, tT):
    M = tT * B
    state = S5_PAIRS * LANES
    kern = functools.partial(_s5_kernel, B=B, tT=tT)
    tile = lambda i: (i, 0, 0)
    return pl.pallas_call(
        kern,
        grid=(L // tT,),
        in_specs=[
            pl.BlockSpec((tT, B, B_WIDTH), tile),
            _const_spec((S5_PAIRS, LANES)),
            _const_spec((S5_PAIRS, LANES)),
            _const_spec((S5_PAIRS, LANES)),
            _const_spec((S5_PAIRS, LANES, LANES)),
            _const_spec((S5_PAIRS, LANES, LANES)),
            _const_spec((S5_PAIRS, LANES, LANES)),
            _const_spec((S5_PAIRS, LANES, LANES)),
            _const_spec((1, B_WIDTH)),
            _layer_spec(*prm["w_glu"]),
            _const_spec((B, state)),
            _const_spec((B, state)),
        ],
        out_specs=[
            pl.BlockSpec((tT, B, B_WIDTH), tile),
            pl.BlockSpec((B, state), lambda i: (0, 0)),
            pl.BlockSpec((B, state), lambda i: (0, 0)),
        ],
        out_shape=[
            jax.ShapeDtypeStruct((L, B, B_WIDTH), BF16 if B % 16 == 0 else F32),
            jax.ShapeDtypeStruct((B, state), F32),
            jax.ShapeDtypeStruct((B, state), F32),
        ],
        scratch_shapes=[
            pltpu.VMEM((2, S5_PAIRS, LANES), F32),
            pltpu.VMEM((S5_PAIRS, LANES, 2 * LANES), BF16),
            pltpu.VMEM((S5_PAIRS, 2 * LANES, LANES), BF16),
            pltpu.VMEM((S5_PAIRS, M, LANES), F32),
            pltpu.VMEM((S5_PAIRS, M, LANES), F32),
        ],
        compiler_params=_params("arbitrary"),
        name="s5_scan",
    )(u_tm.reshape(L, B, B_WIDTH), prm["a_re"], prm["a_im"], prm["ldt"], prm["b_re"], prm["b_im"],
      prm["c_re"], prm["c_im"], prm["d"], prm["w_glu"][0], h0r, h0i)


def _odd_in_kernel(*refs, bB, tT, lc, kv_channel_major, n_alias):
    (h_ref, g_ref, w_ref, wkv_ref, wft_ref, bft_ref, gv_ref, ws_ref, bs_ref) = refs[:9]
    outs = refs[9 + n_alias:]
    if kv_channel_major:
        q_ref, k_ref, v_ref, kb_ref, vb_ref, lf_ref, ft_ref, yd_ref, carry_ref = outs
        vn_ref = None
    else:
        q_ref, k_ref, v_ref, kb_ref, vb_ref, lf_ref, ft_ref, yd_ref, vn_ref, carry_ref = outs
    M = bB * tT

    @pl.when(pl.program_id(1) == 0)
    def _():
        carry_ref[...] = jnp.zeros_like(carry_ref)

    x = h_ref[...].reshape(M, D_MODEL)
    hn = _rms(x, g_ref[...]).astype(BF16)
    z = _dot(hn, w_ref[...])
    q = z[:, 0:C_WIDTH] * (C_HEAD_DIM ** -0.5 * LOG2E)
    u = z[:, C_WIDTH:C_WIDTH + D_WIDTH]
    vd = z[:, C_WIDTH + D_WIDTH:C_WIDTH + 2 * D_WIDTH]
    q_ref[...] = q.reshape(bB, tT, C_WIDTH).astype(BF16)
    if kv_channel_major:
        kv = _dot_nt(wkv_ref[...], hn)
        k_ref[0, 0] = kv[0:C_WIDTH]
        v_ref[0, 0] = kv[C_WIDTH:]
        kb_ref[0] = kv[0:C_WIDTH].astype(BF16)
        ones = jnp.ones((C_HEAD_DIM, M), BF16)
        for hh in range(C_HEADS):
            v_at = hh * LANES + (hh % 2) * C_HEAD_DIM
            ones_at = hh * LANES + (1 - hh % 2) * C_HEAD_DIM
            vb_ref[0, v_at:v_at + C_HEAD_DIM, :] = (
                kv[C_WIDTH + hh * C_HEAD_DIM:C_WIDTH + (hh + 1) * C_HEAD_DIM].astype(BF16))
            vb_ref[0, ones_at:ones_at + C_HEAD_DIM, :] = ones
    else:
        k = z[:, C_WIDTH + 2 * D_WIDTH:2 * C_WIDTH + 2 * D_WIDTH].reshape(bB, tT, C_WIDTH)
        v = z[:, 2 * C_WIDTH + 2 * D_WIDTH:].reshape(bB, tT, C_WIDTH)
        k_ref[...] = k
        v_ref[...] = v
        kb_ref[...] = k.astype(BF16)
        vb_ref[...] = v.astype(BF16)

    logf_t = jax.nn.log_sigmoid(_dot_nt(wft_ref[...], hn) + bft_ref[...])
    ft = _cumsum_lanes(logf_t, tT) + carry_ref[...]
    for b in range(bB):
        lf_ref[b] = logf_t[:, b * tT:(b + 1) * tT]
        ft_ref[b] = ft[:, b * tT:(b + 1) * tT]
    if bB == 1:
        carry_ref[...] = ft[:, M - 1:M]

    vn = _rms(vd, gv_ref[...])
    if vn_ref is not None:
        vn_ref[...] = vn.reshape(bB, tT, D_WIDTH)
    vnb = vn.astype(BF16)
    rr = lax.broadcasted_iota(jnp.int32, (lc, lc), 0)
    cc = lax.broadcasted_iota(jnp.int32, (lc, lc), 1)
    wsm = [jnp.where(cc <= rr, ws_ref[hh], 0.0).astype(BF16) for hh in range(C_HEADS)]
    first_head = lax.broadcasted_iota(jnp.int32, (lc, LANES), 1) < C_HEAD_DIM
    bias = bs_ref[...]
    for ch in range(M // lc):
        rows = slice(ch * lc, (ch + 1) * lc)
        cols = []
        for hp in range(HEAD_PAIRS):
            vp = vnb[rows, hp * LANES:(hp + 1) * LANES]
            cols.append(jnp.where(first_head, _dot(wsm[2 * hp], vp), _dot(wsm[2 * hp + 1], vp)))
        mixed = jnp.concatenate(cols, axis=-1) + bias
        b, off = divmod(ch * lc, tT)
        yd_ref[b, off:off + lc, :] = (u[rows] * mixed).astype(BF16)


def _odd_in(h, lw, bB, tT, j, kv_stacks):
    B, L, _ = h.shape
    nT = L // tT
    assert bB == 1 or nT == 1
    kv_channel_major = kv_stacks is not None
    lc = lw["w_s"].shape[-1]
    tile = lambda b, t: (b, t, 0)
    chan = lambda b, t: (b, 0, t)

    def act(width, dtype):
        return pl.BlockSpec((bB, tT, width), tile), jax.ShapeDtypeStruct((B, L, width), dtype)

    heads = (pl.BlockSpec((bB, C_HEADS, tT), chan), jax.ShapeDtypeStruct((B, C_HEADS, L), F32))
    if kv_channel_major:
        assert bB == 1
        stack = (pl.BlockSpec((1, 1, C_WIDTH, tT), lambda b, t: (j, b, 0, t)),
                 jax.ShapeDtypeStruct((N_ODD, B, C_WIDTH, L), F32))
        chan_bf = (pl.BlockSpec((1, C_WIDTH, tT), chan), jax.ShapeDtypeStruct((B, C_WIDTH, L), BF16))
        chan_v = (pl.BlockSpec((1, C_HEADS * LANES, tT), chan), jax.ShapeDtypeStruct((B, C_HEADS * LANES, L), BF16))
        outs = [act(C_WIDTH, BF16), stack, stack, chan_bf, chan_v, heads, heads, act(D_WIDTH, BF16)]
        w_kv_spec = _const_spec((2 * C_WIDTH, D_MODEL))
    else:
        outs = [act(C_WIDTH, BF16), act(C_WIDTH, F32), act(C_WIDTH, F32), act(C_WIDTH, BF16), act(C_WIDTH, BF16),
                heads, heads, act(D_WIDTH, BF16), act(D_WIDTH, F32)]
        w_kv_spec = _const_spec((1, 1))
    alias_in = list(kv_stacks) if kv_channel_major else []
    n_fixed = 9
    kern = functools.partial(_odd_in_kernel, bB=bB, tT=tT, lc=lc, kv_channel_major=kv_channel_major,
                             n_alias=len(alias_in))
    w_in = lw["w_in_cm"] if kv_channel_major else lw["w_in_rm"]
    w_kv = lw["w_kv_t"] if kv_channel_major else jnp.zeros((1, 1), BF16)
    return pl.pallas_call(
        kern,
        grid=(B // bB, nT),
        in_specs=[
            pl.BlockSpec((bB, tT, D_MODEL), tile),
            _layer_spec(*lw["g_pre"]),
            _const_spec(w_in.shape),
            w_kv_spec,
            _const_spec((C_HEADS, D_MODEL)),
            _const_spec((C_HEADS, 1)),
            _const_spec((1, D_WIDTH)),
            _const_spec((C_HEADS, lc, lc)),
            _const_spec((lc, D_WIDTH)),
        ] + [pl.BlockSpec(memory_space=pl.ANY)] * len(alias_in),
        out_specs=[o[0] for o in outs],
        out_shape=[o[1] for o in outs],
        input_output_aliases={n_fixed + i: 1 + i for i in range(len(alias_in))},
        scratch_shapes=[pltpu.VMEM((C_HEADS, 1), F32)],
        compiler_params=_params("arbitrary", "arbitrary"),
        name="odd_in",
    )(h, lw["g_pre"][0], w_in, w_kv, lw["w_ft"], lw["b_ft"], lw["g_v"], lw["w_s"], lw["b_s"], *alias_in)


def _attn_prompt_kernel(qi_ref, kj_ref, q_ref, kt_ref, vt_ref, ft_ref, o_ref, m_ref, acc_ref, *, tq):
    p = pl.program_id(1)
    qi = qi_ref[p]
    kj = kj_ref[p]
    first_head = lax.broadcasted_iota(jnp.int32, (tq, LANES), 1) < C_HEAD_DIM

    @pl.when(kj == 0)
    def _():
        m_ref[...] = jnp.full_like(m_ref, -jnp.inf)
        acc_ref[...] = jnp.zeros_like(acc_ref)

    def block(diagonal):
        if diagonal:
            visible = (lax.broadcasted_iota(jnp.int32, (tq, tq), 1)
                       <= lax.broadcasted_iota(jnp.int32, (tq, tq), 0))
        fk = ft_ref[0] * LOG2E
        for hp in range(HEAD_PAIRS):
            lanes = slice(hp * LANES, (hp + 1) * LANES)
            qp = q_ref[0, :, lanes]
            kt = kt_ref[0, lanes, :]
            qs = (jnp.where(first_head, qp, jnp.zeros_like(qp)), jnp.where(first_head, jnp.zeros_like(qp), qp))
            done = []
            for e in range(2):
                h = 2 * hp + e
                s = _dot(qs[e], kt) - fk[h:h + 1]
                if diagonal:
                    s = jnp.where(visible, s, MASK_VALUE)
                m_old = m_ref[h]
                m_new = jnp.maximum(m_old, jnp.max(s, axis=-1, keepdims=True))
                a = jnp.exp2(m_old - m_new)
                pr = jnp.exp2(s - jnp.concatenate([m_new] * (tq // LANES), axis=-1))
                acc = acc_ref[h] * a + _dot_nt(pr.astype(BF16), vt_ref[0, h * LANES:(h + 1) * LANES, :])
                if diagonal:
                    done.append(acc)
                else:
                    m_ref[h] = m_new
                    acc_ref[h] = acc
            if diagonal:
                num = jnp.where(first_head, done[0], done[1])
                den = pltpu.roll(jnp.where(first_head, done[1], done[0]), C_HEAD_DIM, axis=1)
                o_ref[0, :, lanes] = (num / den).astype(BF16)

    @pl.when(kj < qi)
    def _():
        block(False)

    @pl.when(kj == qi)
    def _():
        block(True)


def _attn_prompt(q, ktb, vtb, ft, tq):
    B, L, _ = q.shape
    nq = L // tq
    pairs = [(i, k) for i in range(nq) for k in range(i + 1)]
    qi = jnp.asarray(np.array([p[0] for p in pairs], np.int32))
    kj = jnp.asarray(np.array([p[1] for p in pairs], np.int32))
    kern = functools.partial(_attn_prompt_kernel, tq=tq)
    q_spec = pl.BlockSpec((1, tq, C_WIDTH), lambda b, p, qi, kj: (b, qi[p], 0))
    return pl.pallas_call(
        kern,
        grid_spec=pltpu.PrefetchScalarGridSpec(
            num_scalar_prefetch=2,
            grid=(B, len(pairs)),
            in_specs=[q_spec,
                      pl.BlockSpec((1, C_WIDTH, tq), lambda b, p, qi, kj: (b, 0, kj[p])),
                      pl.BlockSpec((1, C_HEADS * LANES, tq), lambda b, p, qi, kj: (b, 0, kj[p])),
                      pl.BlockSpec((1, C_HEADS, tq), lambda b, p, qi, kj: (b, 0, kj[p]))],
            out_specs=q_spec,
            scratch_shapes=[pltpu.VMEM((C_HEADS, tq, LANES), F32), pltpu.VMEM((C_HEADS, tq, LANES), F32)],
        ),
        out_shape=jax.ShapeDtypeStruct((B, L, C_WIDTH), BF16),
        compiler_params=_params("arbitrary", "arbitrary"),
        name="attn_prompt",
    )(qi, kj, q, ktb, vtb, ft)


def _attn_sample_kernel(q_ref, kn_ref, vn_ref, kpt_ref, vpt_ref, lpt_ref, ft_ref, o_ref, *, L, P):
    first_head = lax.broadcasted_iota(jnp.int32, (L, LANES), 1) < C_HEAD_DIM
    past_cum = _cumsum_lanes(lpt_ref[0, 0], P)
    past_after = (past_cum[:, P - 1:P] - past_cum) * LOG2E
    causal = lax.broadcasted_iota(jnp.int32, (L, L), 1) <= lax.broadcasted_iota(jnp.int32, (L, L), 0)
    ft = ft_ref[0] * LOG2E
    for hp in range(HEAD_PAIRS):
        lanes = slice(hp * LANES, (hp + 1) * LANES)
        qp = q_ref[0, :, lanes]
        qs = (jnp.where(first_head, qp, jnp.zeros_like(qp)), jnp.where(first_head, jnp.zeros_like(qp), qp))
        kpt = kpt_ref[0, 0, lanes, :].astype(BF16)
        vpt = vpt_ref[0, 0, lanes, :].astype(BF16)
        kn = kn_ref[0, :, lanes]
        vn = vn_ref[0, :, lanes]
        outs = []
        for e in range(2):
            h = 2 * hp + e
            sp = _dot(qs[e], kpt) + past_after[h:h + 1]
            sn = jnp.where(causal, _dot_nt(qs[e], kn) - ft[h:h + 1], MASK_VALUE)
            m = jnp.maximum(jnp.max(sp, axis=-1, keepdims=True), jnp.max(sn, axis=-1, keepdims=True))
            pp = jnp.exp2(sp - m)
            pn = jnp.exp2(sn - m)
            den = jnp.sum(pp, axis=-1, keepdims=True) + jnp.sum(pn, axis=-1, keepdims=True)
            outs.append((_dot_nt(pp.astype(BF16), vpt) + _dot(pn.astype(BF16), vn)) / den)
        o_ref[0, :, lanes] = jnp.where(first_head, outs[0], outs[1]).astype(BF16)


def _attn_sample(q, kb, vb, k_past_t, v_past_t, logf_past_t, ft, j):
    B, L, _ = q.shape
    P = k_past_t.shape[-1]
    kern = functools.partial(_attn_sample_kernel, L=L, P=P)
    new = pl.BlockSpec((1, L, C_WIDTH), lambda b: (b, 0, 0))
    past = pl.BlockSpec((1, 1, C_WIDTH, P), lambda b: (j, b, 0, 0))
    return pl.pallas_call(
        kern,
        grid=(B,),
        in_specs=[new, new, new, past, past,
                  pl.BlockSpec((1, 1, C_HEADS, P), lambda b: (j, b, 0, 0)),
                  pl.BlockSpec((1, C_HEADS, L), lambda b: (b, 0, 0))],
        out_specs=new,
        out_shape=jax.ShapeDtypeStruct((B, L, C_WIDTH), BF16),
        compiler_params=_params("arbitrary"),
        name="attn_sample",
    )(q, kb, vb, k_past_t, v_past_t, logf_past_t, ft)


def _ffn_kernel(h_ref, a_ref, b_ref, wo_ref, gpost_ref, gpre_ref, wup_ref, wcv_ref, fh_ref, wdn_ref, gfp_ref,
                wgate_ref, pe_ref, wple_ref, o_ref, nf_ref, cb_ref, act_ref, *, bB, tT, b_time_major):
    M = bB * tT

    @pl.when(pl.program_id(1) == 0)
    def _():
        nf_ref[...] = fh_ref[0]

    a = a_ref[...].reshape(M, a_ref.shape[-1])
    if b_time_major:
        width = b_ref.shape[-1] // bB
        bpart = jnp.concatenate([b_ref[:, i * width:(i + 1) * width] for i in range(bB)], axis=0)
    else:
        bpart = b_ref[...].reshape(M, b_ref.shape[-1])
    y = _dot(jnp.concatenate([a.astype(BF16), bpart.astype(BF16)], axis=-1), wo_ref[...])
    h1 = h_ref[...].reshape(M, D_MODEL) + _rms(y, gpost_ref[...])
    hn = _rms(h1, gpre_ref[...]).astype(BF16)

    def up_proj(c):
        return [_dot(hn, wup_ref[:, col0:col0 + FF_CHUNK]).reshape(bB, tT, FF_CHUNK)
                for col0 in (c * FF_CHUNK, D_FF + c * FF_CHUNK)]

    ups = up_proj(0)
    for c in range(N_FF_CHUNKS):
        nxt = up_proj(c + 1) if c + 1 < N_FF_CHUNKS else None
        cb = cb_ref.at[c % 2]
        halves = []
        for half, col0 in enumerate((c * FF_CHUNK, D_FF + c * FF_CHUNK)):
            cols = slice(col0, col0 + FF_CHUNK)
            uc, last = _causal_conv3(cb.at[:, :, half * FF_CHUNK:(half + 1) * FF_CHUNK],
                                     ups[half], nf_ref[:, :, cols], wcv_ref[:, cols], tT)
            nf_ref[:, :, cols] = last
            halves.append(uc)
        act_ref[:, c * FF_CHUNK:(c + 1) * FF_CHUNK] = (
            (_gelu_tanh(halves[0]) * halves[1]).reshape(M, FF_CHUNK).astype(BF16))
        ups = nxt

    h2 = h1 + _rms(_dot(act_ref[...], wdn_ref[...]), gfp_ref[...])
    gate = jax.nn.sigmoid(_dot(h2.astype(BF16), wgate_ref[...]))
    pe = pe_ref[0].reshape(M, PLE_DIM).astype(BF16)
    h3 = h2 + gate * _dot(pe, wple_ref[...])
    o_ref[...] = h3.reshape(bB, tT, D_MODEL)


def _ffn(h, a, b, b_time_major, lw, fh, pe, layer, bB, tT):
    B, L, _ = h.shape
    grid = (B // bB, L // tT)
    kern = functools.partial(_ffn_kernel, bB=bB, tT=tT, b_time_major=b_time_major)
    tile = lambda bi, t: (bi, t, 0)
    wa = a.shape[-1]
    if b_time_major:
        wb = b.shape[-1] // B
        b_spec = pl.BlockSpec((tT, bB * wb), lambda bi, t: (t, bi))
    else:
        wb = b.shape[-1]
        b_spec = pl.BlockSpec((bB, tT, wb), tile)
    fslot = layer if fh.shape[0] > 1 else 0
    return pl.pallas_call(
        kern,
        grid=grid,
        in_specs=[
            pl.BlockSpec((bB, tT, D_MODEL), tile),
            pl.BlockSpec((bB, tT, wa), tile),
            b_spec,
            _layer_spec(*lw["w_out"]),
            _layer_spec(*lw["g_post"]),
            _layer_spec(*lw["g_ffn_pre"]),
            _layer_spec(*lw["w_up"]),
            _layer_spec(*lw["w_fconv"]),
            pl.BlockSpec((1, bB, HIST, 2 * D_FF), lambda bi, t: (fslot, bi, 0, 0)),
            _layer_spec(*lw["w_down"]),
            _layer_spec(*lw["g_ffn_post"]),
            _layer_spec(*lw["w_gate"]),
            pl.BlockSpec((1, bB, tT, PLE_DIM), lambda bi, t: (layer, bi, t, 0)),
            _layer_spec(*lw["w_ple"]),
        ],
        out_specs=[pl.BlockSpec((bB, tT, D_MODEL), tile),
                   pl.BlockSpec((bB, HIST, 2 * D_FF), lambda bi, t: (bi, 0, 0))],
        out_shape=[
            jax.ShapeDtypeStruct((B, L, D_MODEL), F32),
            jax.ShapeDtypeStruct((B, HIST, 2 * D_FF), F32),
        ],
        scratch_shapes=[
            pltpu.VMEM((2, bB, (tT if bB > 1 else 0) + CONV_PAD, 2 * FF_CHUNK), F32),
            pltpu.VMEM((bB * tT, D_FF), BF16),
        ],
        compiler_params=_params("arbitrary", "arbitrary"),
        name="mix_out_ffn",
    )(h, a, b, lw["w_out"][0], lw["g_post"][0], lw["g_ffn_pre"][0], lw["w_up"][0], lw["w_fconv"][0], fh,
      lw["w_down"][0], lw["g_ffn_post"][0], lw["w_gate"][0], pe, lw["w_ple"][0])


def _s5_block_layout(x, state_major):
    per_quad = S5_PAIRS // S5_QUADS
    eye2 = jnp.eye(2, dtype=x.dtype)
    slot = jax.nn.one_hot(jnp.arange(S5_PAIRS) % per_quad, per_quad, dtype=x.dtype)
    if state_major:
        x4 = x.reshape(S5_PAIRS, 2, S5_STATE, S5_GROUP)
        blk = jnp.einsum("jgpc,gh->jgchp", x4, eye2).reshape(S5_PAIRS, 2 * S5_GROUP, LANES)
        return jnp.einsum("jkn,jr->jrkn", blk, slot).reshape(S5_PAIRS, LANES, LANES)
    x4 = x.reshape(S5_PAIRS, 2, S5_GROUP, S5_STATE)
    blk = jnp.einsum("jgcp,gh->jgphc", x4, eye2).reshape(S5_PAIRS, LANES, 2 * S5_GROUP)
    return jnp.einsum("jnk,jr->jnrk", blk, slot).reshape(S5_PAIRS, LANES, LANES)


STACKED_MATMUL_WEIGHTS = ("w_ffn_up", "w_ffn_down", "w_ple_gate", "w_ple", "w_even_in", "w_even_out", "w_glu",
                          "w_odd_out")


def _layer_params(i, W, WB):
    j = i // 2
    lw = {
        "g_pre": (WB["g_mix_pre"], i), "g_post": (WB["g_mix_post"], i),
        "g_ffn_pre": (WB["g_ffn_pre"], i), "g_ffn_post": (WB["g_ffn_post"], i),
        "w_up": (WB["w_ffn_up"], i), "w_fconv": (W["w_ffn_conv"], i), "w_down": (WB["w_ffn_down"], i),
        "w_gate": (WB["w_ple_gate"], i), "w_ple": (WB["w_ple"], i),
    }
    if i % 2 == 0:
        lw.update({
            "w_in": (WB["w_even_in"], j), "w_conv": (W["w_conv_a"], j), "w_out": (WB["w_even_out"], j),
            "s5": {
                "a_re": W["s5_a_re"][j].reshape(S5_PAIRS, LANES),
                "a_im": W["s5_a_im"][j].reshape(S5_PAIRS, LANES),
                "ldt": jnp.broadcast_to(W["s5_log_dt"][j][:, None], (S5_GROUPS, S5_STATE)).reshape(S5_PAIRS, LANES),
                "b_re": _s5_block_layout(W["s5_b_re"][j], True), "b_im": _s5_block_layout(W["s5_b_im"][j], True),
                "c_re": _s5_block_layout(W["s5_c_re"][j], False), "c_im": _s5_block_layout(W["s5_c_im"][j], False),
                "d": W["s5_d"][j].reshape(1, B_WIDTH), "w_glu": (WB["w_glu"], j),
            },
        })
    else:
        w = W["w_odd_in"][j].astype(BF16)
        f0 = 3 * C_WIDTH
        w_q, w_kv, w_f, w_ud = w[:, :C_WIDTH], w[:, C_WIDTH:f0], w[:, f0:f0 + C_HEADS], w[:, f0 + C_HEADS:]
        lw.update({
            "w_in_cm": jnp.concatenate([w_q, w_ud], axis=1),
            "w_in_rm": jnp.concatenate([w_q, w_ud, w_kv], axis=1),
            "w_kv_t": w_kv.T, "w_ft": w_f.T, "b_ft": W["b_forget"][j][:, None],
            "g_v": W["g_gmlp_v"][j][None],
            "w_out": (WB["w_odd_out"], j),
        })
        for lc in (GMLP_CHUNK, 32):
            lw[("w_s", lc)] = W["w_spatial"][j][:, :lc, :lc]
            lw[("b_s", lc)] = jnp.repeat(W["b_spatial"][j][:, :lc].T, D_WIDTH // C_HEADS, axis=1)
    return lw


def _run_trunk(x, pe, st, LW, bB, tT, tT_s5, tq, tT_ffn):
    B, L, _ = x.shape
    prompt = st is None
    lc = min(L, GMLP_CHUNK)
    h = x
    conv_a, ssm_re, ssm_im, ks, vs, lfs, gvs, ffs = [], [], [], [], [], [], [], []
    kv_stacks = ()
    fh = jnp.zeros((1, B, HIST, 2 * D_FF), F32) if prompt else st["ffn"]
    for i in range(DEPTH):
        j = i // 2
        lw = LW[i]
        if i % 2 == 0:
            if prompt:
                hist = jnp.zeros((B, HIST, A_WIDTH), F32)
                h0r = jnp.zeros((B, S5_GROUPS * S5_STATE), F32)
                h0i = h0r
            else:
                hist = st["conv_a"][j]
                h0r = st["ssm_re"][j].reshape(B, S5_GROUPS * S5_STATE)
                h0i = st["ssm_im"][j].reshape(B, S5_GROUPS * S5_STATE)
            ya, u_tm, new_hist = _even_in(h, lw["g_pre"], lw["w_in"], lw["w_conv"], hist, bB, tT)
            yb_tm, nr, ni = _s5(u_tm, lw["s5"], h0r, h0i, B, L, tT_s5)
            a, b, b_tm = ya, yb_tm.reshape(L, B * B_WIDTH), True
            conv_a.append(new_hist)
            ssm_re.append(nr.reshape(B, S5_GROUPS, S5_STATE))
            ssm_im.append(ni.reshape(B, S5_GROUPS, S5_STATE))
        else:
            lw = dict(lw, w_s=lw[("w_s", lc)], b_s=lw[("b_s", lc)])
            if prompt:
                q, k_stack, v_stack, ktb, vtb, logf_t, ft, yd = _odd_in(h, lw, bB, tT, j, kv_stacks)
                kv_stacks = (k_stack, v_stack)
                att = _attn_prompt(q, ktb, vtb, ft, tq)
            else:
                q, k, v, kb, vb, logf_t, ft, yd, vn = _odd_in(h, lw, bB, tT, j, None)
                att = _attn_sample(q, kb, vb, st["k_t"], st["v_t"], st["logf_t"], ft, j)
                ks.append(k.reshape(B, L, C_HEADS, C_HEAD_DIM))
                vs.append(v.reshape(B, L, C_HEADS, C_HEAD_DIM))
                gvs.append(vn)
            a, b, b_tm = att, yd, False
            lfs.append(logf_t)
        h, new_f = _ffn(h, a, b, b_tm, lw, fh, pe, i, bB, tT_ffn)
        ffs.append(new_f)
    new_state = {"conv_a": jnp.stack(conv_a), "ssm_re": jnp.stack(ssm_re), "ssm_im": jnp.stack(ssm_im),
                 "logf": jnp.swapaxes(jnp.stack(lfs), 2, 3), "ffn": jnp.stack(ffs)}
    if prompt:
        for name, stack in zip(("k", "v"), kv_stacks):
            new_state[name] = jnp.transpose(stack.reshape(N_ODD, B, C_HEADS, C_HEAD_DIM, L), (0, 1, 4, 2, 3))
    else:
        new_state.update({"k": jnp.stack(ks), "v": jnp.stack(vs), "gmlp_v": jnp.stack(gvs)})
    return h, new_state


def kernel(x_prompt, x_sample, p_prompt, p_sample, cache_conv_a, state_ssm_re, state_ssm_im, cache_k, cache_v, cache_logf, cache_ffn_conv, g_mix_pre, g_mix_post, g_ffn_pre, g_ffn_post, w_even_in, w_conv_a, s5_a_re, s5_a_im, s5_log_dt, s5_b_re, s5_b_im, s5_c_re, s5_c_im, s5_d, w_glu, w_even_out, w_odd_in, b_forget, w_spatial, b_spatial, g_gmlp_v, w_odd_out, w_ffn_up, w_ffn_conv, w_ffn_down, w_ple, w_ple_gate):
    W = {"g_mix_pre": g_mix_pre, "g_mix_post": g_mix_post, "g_ffn_pre": g_ffn_pre, "g_ffn_post": g_ffn_post,
         "w_even_in": w_even_in, "w_conv_a": w_conv_a, "s5_a_re": s5_a_re, "s5_a_im": s5_a_im,
         "s5_log_dt": s5_log_dt, "s5_b_re": s5_b_re, "s5_b_im": s5_b_im, "s5_c_re": s5_c_re,
         "s5_c_im": s5_c_im, "s5_d": s5_d, "w_glu": w_glu, "w_even_out": w_even_out,
         "w_odd_in": w_odd_in, "b_forget": b_forget, "w_spatial": w_spatial, "b_spatial": b_spatial,
         "g_gmlp_v": g_gmlp_v, "w_odd_out": w_odd_out, "w_ffn_up": w_ffn_up, "w_ffn_conv": w_ffn_conv,
         "w_ffn_down": w_ffn_down, "w_ple": w_ple, "w_ple_gate": w_ple_gate}
    WB = {name: W[name].astype(BF16) for name in STACKED_MATMUL_WEIGHTS}
    for name in ("g_mix_pre", "g_mix_post", "g_ffn_pre", "g_ffn_post"):
        WB[name] = W[name].reshape(DEPTH, 1, D_MODEL)
    LW = [_layer_params(i, W, WB) for i in range(DEPTH)]
    y_prompt, sp = _run_trunk(x_prompt, p_prompt, None, LW, bB=1, tT=1024, tT_s5=64, tq=512, tT_ffn=512)
    dec_b, dec_l, _ = x_sample.shape
    past = cache_k.shape[2]
    st = {"conv_a": cache_conv_a, "ssm_re": state_ssm_re, "ssm_im": state_ssm_im, "ffn": cache_ffn_conv,
          "k_t": jnp.transpose(cache_k, (0, 1, 3, 4, 2)).reshape(N_ODD, dec_b, C_WIDTH, past),
          "v_t": jnp.transpose(cache_v, (0, 1, 3, 4, 2)).reshape(N_ODD, dec_b, C_WIDTH, past),
          "logf_t": jnp.swapaxes(cache_logf, 2, 3)}
    y_sample, ss = _run_trunk(x_sample, p_sample, st, LW, bB=dec_b, tT=dec_l, tT_s5=dec_l, tq=dec_l, tT_ffn=dec_l)
    return (y_prompt, y_sample,
            sp["conv_a"], sp["ssm_re"], sp["ssm_im"], sp["k"], sp["v"], sp["logf"], sp["ffn"],
            ss["conv_a"], ss["ssm_re"], ss["ssm_im"], ss["k"], ss["v"], ss["logf"], ss["gmlp_v"], ss["ffn"])
```
